```python
import jax, jax.numpy as jnp
from jax import lax
import numpy as np

D_MODEL = 1024
BATCH = 8
SEQ = 4096
DEPTH = 4

HEAD_DIM = 64
SB_HEADS = D_MODEL // HEAD_DIM
SW_Q_HEADS = D_MODEL // HEAD_DIM
SW_KV_HEADS = max(1, SW_Q_HEADS // 8)
SW_GROUP = SW_Q_HEADS // SW_KV_HEADS
SW_IN = (SW_Q_HEADS + 2 * SW_KV_HEADS) * HEAD_DIM
WINDOW = 128
BLOCK = 128
ROPE_THETA = 500000.0
ROT_DIM = HEAD_DIM // 4
D_FF = ((8 * D_MODEL // 3 + 255) // 256) * 256
N_MIXERS = 2
N_SB_LAYERS = (DEPTH + 1) // 2
N_SW_LAYERS = DEPTH // 2
EPS = 1e-6

kernel_name = 'hybrid_stickbreak_swa_macaron'


def rms_norm(x, gain):
    xf = x.astype(jnp.float32)
    y = xf * lax.rsqrt(jnp.mean(xf * xf, axis=-1, keepdims=True) + EPS)
    return (y * gain.astype(jnp.float32)).astype(x.dtype)


def swiglu(h, w_gate_up, w_down):
    gate, up = jnp.split(h @ w_gate_up, 2, axis=-1)
    return (jax.nn.silu(gate) * up) @ w_down


def rope_angles(positions):
    inv_freq = ROPE_THETA ** (-jnp.arange(0, ROT_DIM, 2, dtype=jnp.float32) / ROT_DIM)
    ang = positions.astype(jnp.float32)[..., None] * inv_freq
    return jnp.cos(ang), jnp.sin(ang)


def apply_partial_rope(x, cos, sin):
    half = ROT_DIM // 2
    x1 = x[..., :half].astype(jnp.float32)
    x2 = x[..., half:ROT_DIM].astype(jnp.float32)
    rot = jnp.concatenate([x1 * cos - x2 * sin, x2 * cos + x1 * sin], axis=-1).astype(x.dtype)
    return jnp.concatenate([rot, x[..., ROT_DIM:]], axis=-1)


def stick_breaking_attention(h, w_in, w_out):
    b, s, _ = h.shape
    q, k, v = jnp.split(h @ w_in, 3, axis=-1)
    q = q.reshape(b, s, SB_HEADS, HEAD_DIM).transpose(0, 2, 1, 3)
    k = k.reshape(b, s, SB_HEADS, HEAD_DIM).transpose(0, 2, 1, 3)
    v = v.reshape(b, s, SB_HEADS, HEAD_DIM).transpose(0, 2, 1, 3)
    scale = HEAD_DIM ** -0.5
    outs = []
    for blk in range(s // BLOCK):
        t0 = blk * BLOCK
        kv_len = t0 + BLOCK
        qb = q[:, :, t0:kv_len]
        kb = k[:, :, :kv_len]
        vb = v[:, :, :kv_len]
        z = jnp.einsum('bhtd,bhsd->bhts', qb, kb).astype(jnp.float32) * scale
        causal = (np.arange(kv_len)[None, :] < (t0 + np.arange(BLOCK))[:, None])
        log_beta = jax.nn.log_sigmoid(z)
        log_keep = jnp.where(causal, log_beta - z, 0.0)
        later = lax.cumsum(log_keep, axis=3, reverse=True) - log_keep
        weights = jnp.where(causal, jnp.exp(log_beta + later), 0.0)
        outs.append(jnp.einsum('bhts,bhsd->bhtd', weights.astype(vb.dtype), vb))
    o = jnp.concatenate(outs, axis=2).transpose(0, 2, 1, 3).reshape(b, s, SB_HEADS * HEAD_DIM)
    return o @ w_out


def sliding_window_attention(h, positions, w_in, w_out, q_gain, k_gain, sinks):
    b, s, _ = h.shape
    nb = s // BLOCK
    qkv = h @ w_in
    nq = SW_Q_HEADS * HEAD_DIM
    nk = SW_KV_HEADS * HEAD_DIM
    q = qkv[..., :nq].reshape(b, s, SW_KV_HEADS, SW_GROUP, HEAD_DIM)
    k = qkv[..., nq:nq + nk].reshape(b, s, SW_KV_HEADS, HEAD_DIM)
    v = qkv[..., nq + nk:].reshape(b, s, SW_KV_HEADS, HEAD_DIM)
    q = rms_norm(q, q_gain)
    k = rms_norm(k, k_gain)
    cos, sin = rope_angles(positions)
    q = apply_partial_rope(q, cos[:, :, None, None, :], sin[:, :, None, None, :])
    k = apply_partial_rope(k, cos[:, :, None, :], sin[:, :, None, :])
    pad = jnp.zeros((b, BLOCK, SW_KV_HEADS, HEAD_DIM), k.dtype)
    k_pad = jnp.concatenate([pad, k], axis=1).reshape(b, nb + 1, BLOCK, SW_KV_HEADS, HEAD_DIM)
    v_pad = jnp.concatenate([pad, v], axis=1).reshape(b, nb + 1, BLOCK, SW_KV_HEADS, HEAD_DIM)
    k_band = jnp.concatenate([k_pad[:, :-1], k_pad[:, 1:]], axis=2)
    v_band = jnp.concatenate([v_pad[:, :-1], v_pad[:, 1:]], axis=2)
    qb = q.reshape(b, nb, BLOCK, SW_KV_HEADS, SW_GROUP, HEAD_DIM)
    scores = jnp.einsum('bntkgd,bnskd->bnkgts', qb, k_band).astype(jnp.float32) * (HEAD_DIM ** -0.5)
    blk_start = np.arange(nb)[:, None, None] * BLOCK
    q_pos = blk_start + np.arange(BLOCK)[None, :, None]
    k_pos = blk_start - BLOCK + np.arange(2 * BLOCK)[None, None, :]
    mask = (k_pos >= 0) & (k_pos <= q_pos) & (q_pos - k_pos < WINDOW)
    scores = jnp.where(mask[None, :, None, None], scores, -jnp.inf)
    sink = sinks.astype(jnp.float32).reshape(SW_KV_HEADS, SW_GROUP)[None, None, :, :, None, None]
    m = jnp.maximum(jnp.max(scores, axis=-1, keepdims=True), sink)
    p = jnp.exp(scores - m)
    probs = p / (jnp.sum(p, axis=-1, keepdims=True) + jnp.exp(sink - m))
    o = jnp.einsum('bnkgts,bnskd->bntkgd', probs.astype(v_band.dtype), v_band)
    return o.reshape(b, s, SW_Q_HEADS * HEAD_DIM) @ w_out


def setup_inputs(seed: int = 0) -> dict:
    key = jax.random.key(seed)
    ks = jax.random.split(key, 14)
    f32 = jnp.float32
    x = jax.random.normal(ks[0], (BATCH, SEQ, D_MODEL), f32)
    offsets = jax.random.randint(ks[1], (BATCH, 1), 0, 1024, dtype=jnp.int32)
    positions = offsets + jnp.arange(SEQ, dtype=jnp.int32)[None, :]
    norm_gains = 1.0 + 0.02 * jax.random.normal(ks[2], (DEPTH, 3, D_MODEL), f32)
    ffn_w_gate_up = jax.random.normal(ks[3], (DEPTH, 2, D_MODEL, 2 * D_FF), f32) * D_MODEL ** -0.5
    ffn_w_down = jax.random.normal(ks[4], (DEPTH, 2, D_FF, D_MODEL), f32) * D_FF ** -0.5
    sb_w_in = jax.random.normal(ks[5], (N_SB_LAYERS, D_MODEL, 3 * SB_HEADS * HEAD_DIM), f32) * D_MODEL ** -0.5
    sb_w_out = jax.random.normal(ks[6], (N_SB_LAYERS, SB_HEADS * HEAD_DIM, D_MODEL), f32) * (SB_HEADS * HEAD_DIM) ** -0.5
    sw_w_in = jax.random.normal(ks[7], (N_SW_LAYERS, D_MODEL, SW_IN), f32) * D_MODEL ** -0.5
    sw_w_out = jax.random.normal(ks[8], (N_SW_LAYERS, SW_Q_HEADS * HEAD_DIM, D_MODEL), f32) * (SW_Q_HEADS * HEAD_DIM) ** -0.5
    sw_q_gain = 1.0 + 0.02 * jax.random.normal(ks[9], (N_SW_LAYERS, HEAD_DIM), f32)
    sw_k_gain = 1.0 + 0.02 * jax.random.normal(ks[10], (N_SW_LAYERS, HEAD_DIM), f32)
    sw_sinks = 0.5 * jax.random.normal(ks[11], (N_SW_LAYERS, SW_Q_HEADS), f32)
    return {'x': x, 'positions': positions, 'norm_gains': norm_gains,
            'ffn_w_gate_up': ffn_w_gate_up, 'ffn_w_down': ffn_w_down,
            'sb_w_in': sb_w_in, 'sb_w_out': sb_w_out,
            'sw_w_in': sw_w_in, 'sw_w_out': sw_w_out,
            'sw_q_gain': sw_q_gain, 'sw_k_gain': sw_k_gain, 'sw_sinks': sw_sinks}


def reference(x, positions, norm_gains, ffn_w_gate_up, ffn_w_down, sb_w_in, sb_w_out,
              sw_w_in, sw_w_out, sw_q_gain, sw_k_gain, sw_sinks):
    for i in range(DEPTH):
        slot = i // N_MIXERS
        x = x + 0.5 * swiglu(rms_norm(x, norm_gains[i, 0]), ffn_w_gate_up[i, 0], ffn_w_down[i, 0])
        h = rms_norm(x, norm_gains[i, 1])
        if i % N_MIXERS == 0:
            x = x + stick_breaking_attention(h, sb_w_in[slot], sb_w_out[slot])
        else:
            x = x + sliding_window_attention(h, positions, sw_w_in[slot], sw_w_out[slot],
                                             sw_q_gain[slot], sw_k_gain[slot], sw_sinks[slot])
        x = x + 0.5 * swiglu(rms_norm(x, norm_gains[i, 2]), ffn_w_gate_up[i, 1], ffn_w_down[i, 1])
    return x
```

```python
import functools

import jax
import jax.numpy as jnp
from jax import lax
from jax.experimental import pallas as pl
from jax.experimental.pallas import tpu as pltpu

F32 = jnp.float32
BF16 = jnp.bfloat16

HEAD_DIM = 64
LANES = 128
SW_KV_HEADS = 2
SW_GROUP = 8
SW_BLOCK = 128
SB_BLOCK = 256
ROPE_THETA = 500000.0
ROT_DIM = HEAD_DIM // 4
EPS = 1e-6
VMEM_LIMIT_BYTES = 56 * 1024 * 1024
ROW_TILE = 512
FF_TILE = 256


def _params(n_axes):
    return pltpu.CompilerParams(
        dimension_semantics=("arbitrary",) * n_axes,
        vmem_limit_bytes=VMEM_LIMIT_BYTES)


def _resident(shape):
    zeros = (0,) * len(shape)
    return pl.BlockSpec(shape, lambda *_: zeros, pipeline_mode=pl.Buffered(1))


def _rms_norm_rows(x, gain):
    ms = jnp.mean(x * x, axis=-1, keepdims=True)
    return x * lax.rsqrt(ms + EPS) * gain


def _dot(a, b):
    return jnp.dot(a, b, preferred_element_type=F32)


def _dot_nt(a, b):
    return lax.dot_general(a, b, (((1,), (1,)), ((), ())), preferred_element_type=F32)


def _split_bf16(x):
    hi = x.astype(BF16)
    lo = (x - hi.astype(F32)).astype(BF16)
    return hi, lo


def _ffn_kernel(x_ref, g_ref, wgu_ref, wd_ref, o_ref, *, d_ff):
    x = x_ref[...]
    h = _rms_norm_rows(x, g_ref[...]).astype(BF16)
    y = jnp.zeros_like(x)
    for c in range(d_ff // FF_TILE):
        lo = c * FF_TILE
        gate = _dot(h, wgu_ref[:, lo:lo + FF_TILE])
        up = _dot(h, wgu_ref[:, d_ff + lo:d_ff + lo + FF_TILE])
        act = gate * (1.0 / (1.0 + jnp.exp(-gate))) * up
        y = y + _dot(act.astype(BF16), wd_ref[lo:lo + FF_TILE, :])
    o_ref[...] = x + 0.5 * y


def _ffn(x, gain, w_gate_up, w_down):
    m, d = x.shape
    d_ff = w_down.shape[0]
    return pl.pallas_call(
        functools.partial(_ffn_kernel, d_ff=d_ff),
        grid=(m // ROW_TILE,),
        in_specs=[
            pl.BlockSpec((ROW_TILE, d), lambda i: (i, 0)),
            _resident((1, d)),
            _resident((d, 2 * d_ff)),
            _resident((d_ff, d)),
        ],
        out_specs=pl.BlockSpec((ROW_TILE, d), lambda i: (i, 0)),
        out_shape=jax.ShapeDtypeStruct((m, d), F32),
        compiler_params=_params(1),
        name="ffn",
    )(x, gain, w_gate_up, w_down)


def _norm_proj_kernel(x_ref, g_ref, w_ref, o_ref, *, n_scaled, scale):
    h = _rms_norm_rows(x_ref[...], g_ref[...]).astype(BF16)
    y = _dot(h, w_ref[...])
    o_ref[:, :n_scaled] = (y[:, :n_scaled] * scale).astype(o_ref.dtype)
    o_ref[:, n_scaled:] = y[:, n_scaled:].astype(o_ref.dtype)


def _norm_proj(x, gain, w, *, n_scaled, scale):
    m, d = x.shape
    n = w.shape[1]
    return pl.pallas_call(
        functools.partial(_norm_proj_kernel, n_scaled=n_scaled, scale=scale),
        grid=(m // ROW_TILE,),
        in_specs=[
            pl.BlockSpec((ROW_TILE, d), lambda i: (i, 0)),
            _resident((1, d)),
            _resident((d, n)),
        ],
        out_specs=pl.BlockSpec((ROW_TILE, n), lambda i: (i, 0)),
        out_shape=jax.ShapeDtypeStruct((m, n), BF16),
        compiler_params=_params(1),
        name="norm_proj",
    )(x, gain, w)


def _proj_residual_kernel(a_ref, w_ref, x_ref, o_ref):
    o_ref[...] = x_ref[...] + _dot(a_ref[...], w_ref[...])


def _proj_residual(a, w, x):
    m, d = x.shape
    k = a.shape[1]
    return pl.pallas_call(
        _proj_residual_kernel,
        grid=(m // ROW_TILE,),
        in_specs=[
            pl.BlockSpec((ROW_TILE, k), lambda i: (i, 0)),
            _resident((k, d)),
            pl.BlockSpec((ROW_TILE, d), lambda i: (i, 0)),
        ],
        out_specs=pl.BlockSpec((ROW_TILE, d), lambda i: (i, 0)),
        out_shape=jax.ShapeDtypeStruct((m, d), F32),
        compiler_params=_params(1),
        name="proj_residual",
    )(a, w, x)


def _sb_attn_kernel(q_ref, k_ref, v_ref, tri_ref, o_ref):
    blk = SB_BLOCK
    i = pl.program_id(2)
    q = q_ref[...]
    lane = lax.broadcasted_iota(jnp.int32, (blk, LANES), 1)
    first = lane < HEAD_DIM
    zero = jnp.zeros_like(q)
    qs = jnp.concatenate([jnp.where(first, q, zero), jnp.where(first, zero, q)], axis=0)
    tri2 = tri_ref[...]

    def tile(j, carry, causal):
        start = pl.multiple_of(j * blk, blk)
        kb = k_ref[pl.ds(start, blk), :]
        vb = v_ref[pl.ds(start, blk), :]
        z = _dot_nt(qs, kb)
        sp = jnp.maximum(z, 0.0) + jnp.log(1.0 + jnp.exp(-jnp.abs(z)))
        if causal is not None:
            sp = jnp.where(causal, sp, 0.0)
        hi, lo = _split_bf16(sp)
        later = _dot(jnp.concatenate([hi, lo], axis=1), tri2)
        w = jnp.exp(z + later + carry)
        if causal is not None:
            w = jnp.where(causal, w, 0.0)
        pv = _dot(w.astype(BF16), vb)
        return pv, carry + later[:, 0:1]

    row = lax.broadcasted_iota(jnp.int32, (2 * blk, blk), 0)
    row = jnp.where(row >= blk, row - blk, row)
    col = lax.broadcasted_iota(jnp.int32, (2 * blk, blk), 1)
    acc, carry = tile(i, jnp.zeros((2 * blk, 1), F32), col < row)

    def body(t, state):
        acc, carry = state
        pv, carry = tile(i - 1 - t, carry, None)
        return acc + pv, carry

    acc, carry = lax.fori_loop(0, i, body, (acc, carry))
    o_ref[...] = jnp.where(first, acc[:blk], acc[blk:]).astype(o_ref.dtype)


def _sb_attention(qkv, tri2, *, n_heads):
    b, s, _ = qkv.shape
    n_pairs = n_heads * HEAD_DIM // LANES
    blk = SB_BLOCK
    return pl.pallas_call(
        _sb_attn_kernel,
        grid=(b, n_pairs, s // blk),
        in_specs=[
            pl.BlockSpec((None, blk, LANES), lambda bi, h, i: (bi, i, h)),
            pl.BlockSpec((None, s, LANES), lambda bi, h, i: (bi, 0, n_pairs + h)),
            pl.BlockSpec((None, s, LANES), lambda bi, h, i: (bi, 0, 2 * n_pairs + h)),
            _resident((2 * blk, blk)),
        ],
        out_specs=pl.BlockSpec((None, blk, LANES), lambda bi, h, i: (bi, i, h)),
        out_shape=jax.ShapeDtypeStruct((b, s, n_pairs * LANES), BF16),
        compiler_params=_params(3),
        name="sb_attention",
    )(qkv, qkv, qkv, tri2)


def _rope_table_kernel(pos_ref, invf_ref, cos_ref, sin_ref):
    ang = pos_ref[...].astype(F32) * invf_ref[...]
    cos_ref[...] = jnp.cos(ang)
    sin_ref[...] = jnp.sin(ang)


def _rope_tables(positions):
    b, s = positions.shape
    half = ROT_DIM // 2
    rows = s * half // LANES
    inv_freq = ROPE_THETA ** (-jnp.arange(0, ROT_DIM, 2, dtype=F32) / ROT_DIM)
    invf = jnp.tile(inv_freq, LANES // half).reshape(1, LANES)
    pos = jnp.repeat(positions, half, axis=1).reshape(b, rows, LANES)
    spec = pl.BlockSpec((None, rows, LANES), lambda i: (i, 0, 0))
    cos, sin = pl.pallas_call(
        _rope_table_kernel,
        grid=(b,),
        in_specs=[spec, _resident((1, LANES))],
        out_specs=[spec, spec],
        out_shape=[jax.ShapeDtypeStruct((b, rows, LANES), F32)] * 2,
        compiler_params=_params(1),
        name="rope_tables",
    )(pos, invf)
    cos = cos.reshape(b, s, half)
    sin = sin.reshape(b, s, half)
    rest = HEAD_DIM - ROT_DIM
    ones = jnp.ones((b, s, rest), F32)
    zeros = jnp.zeros((b, s, rest + half), F32)
    reps = LANES // HEAD_DIM
    cos_t = jnp.tile(jnp.concatenate([cos, cos, ones], axis=-1), (1, 1, reps))
    sin_up = jnp.tile(jnp.concatenate([sin, zeros], axis=-1), (1, 1, reps))
    sin_dn = jnp.tile(jnp.concatenate([zeros[..., :half], sin, zeros[..., :rest]], axis=-1),
                      (1, 1, reps))
    m = b * s
    return cos_t.reshape(m, LANES), sin_up.reshape(m, LANES), sin_dn.reshape(m, LANES)


def _sw_proj_kernel(x_ref, g_ref, w_ref, cos_ref, sup_ref, sdn_ref, hg_ref, seg_ref, o_ref,
                    *, n_q_groups, n_k_groups, n_groups):
    h = _rms_norm_rows(x_ref[...], g_ref[...]).astype(BF16)
    y = _dot(h, w_ref[...])
    cos_t = cos_ref[...]
    sin_up = sup_ref[...]
    sin_dn = sdn_ref[...]
    seg2 = seg_ref[...]
    for grp in range(n_groups):
        yg = y[:, grp * LANES:(grp + 1) * LANES]
        if grp < n_q_groups + n_k_groups:
            is_q = grp < n_q_groups
            hi, lo = _split_bf16(yg * yg)
            ss = _dot(jnp.concatenate([hi, lo], axis=1), seg2)
            gain = hg_ref[0:1, :] if is_q else hg_ref[1:2, :]
            yn = yg * lax.rsqrt(ss * (1.0 / HEAD_DIM) + EPS) * gain
            up = pltpu.roll(yn, LANES - ROT_DIM // 2, 1)
            dn = pltpu.roll(yn, ROT_DIM // 2, 1)
            r = yn * cos_t - up * sin_up + dn * sin_dn
            if is_q:
                r = r * (HEAD_DIM ** -0.5)
            o_ref[:, grp * LANES:(grp + 1) * LANES] = r.astype(o_ref.dtype)
        else:
            o_ref[:, grp * LANES:(grp + 1) * LANES] = yg.astype(o_ref.dtype)


def _sw_proj(x, gain, w, cos_t, sin_up, sin_dn, head_gains, seg2, *, n_q_groups, n_k_groups):
    m, d = x.shape
    n = w.shape[1]
    row = lambda i: (i, 0)
    return pl.pallas_call(
        functools.partial(_sw_proj_kernel, n_q_groups=n_q_groups, n_k_groups=n_k_groups,
                          n_groups=n // LANES),
        grid=(m // ROW_TILE,),
        in_specs=[
            pl.BlockSpec((ROW_TILE, d), row),
            _resident((1, d)),
            _resident((d, n)),
            pl.BlockSpec((ROW_TILE, LANES), row),
            pl.BlockSpec((ROW_TILE, LANES), row),
            pl.BlockSpec((ROW_TILE, LANES), row),
            _resident((2, LANES)),
            _resident((2 * LANES, LANES)),
        ],
        out_specs=pl.BlockSpec((ROW_TILE, n), row),
        out_shape=jax.ShapeDtypeStruct((m, n), BF16),
        compiler_params=_params(1),
        name="sw_proj",
    )(x, gain, w, cos_t, sin_up, sin_dn, head_gains, seg2)


def _swa_attn_kernel(q_ref, kc_ref, kp_ref, vc_ref, vp_ref, sink_ref, o_ref):
    blk = SW_BLOCK
    n = pl.program_id(1)
    lane = lax.broadcasted_iota(jnp.int32, (blk, LANES), 1)
    first = lane < HEAD_DIM
    t = lax.broadcasted_iota(jnp.int32, (blk, 2 * blk), 0)
    j = lax.broadcasted_iota(jnp.int32, (blk, 2 * blk), 1)
    mask = (j > t) & (j <= t + blk) & ((j >= blk) | (n > 0))
    for kh in range(SW_KV_HEADS):
        cols = slice(kh * LANES, (kh + 1) * LANES)
        kb = jnp.concatenate([kp_ref[:, cols], kc_ref[:, cols]], axis=0)
        vb = jnp.concatenate([vp_ref[:, cols], vc_ref[:, cols]], axis=0)
        rows = []
        for g in range(SW_GROUP):
            pair = (kh * SW_GROUP + g) // 2
            qp = q_ref[:, pair * LANES:(pair + 1) * LANES]
            zero = jnp.zeros_like(qp)
            rows.append(jnp.where(first, qp, zero) if g % 2 == 0 else jnp.where(first, zero, qp))
        s = _dot_nt(jnp.concatenate(rows, axis=0), kb)
        s = s.reshape(SW_GROUP, blk, 2 * blk)
        s = jnp.where(mask[None], s, -jnp.inf)
        sink = sink_ref[kh]
        m = jnp.maximum(jnp.max(s, axis=-1, keepdims=True), sink)
        p = jnp.exp(s - m)
        probs = p / (jnp.sum(p, axis=-1, keepdims=True) + jnp.exp(sink - m))
        o = _dot(probs.reshape(SW_GROUP * blk, 2 * blk).astype(BF16), vb)
        for g2 in range(SW_GROUP // 2):
            pair = kh * (SW_GROUP // 2) + g2
            even = o[(2 * g2) * blk:(2 * g2 + 1) * blk]
            odd = o[(2 * g2 + 1) * blk:(2 * g2 + 2) * blk]
            o_ref[:, pair * LANES:(pair + 1) * LANES] = jnp.where(first, even, odd).astype(o_ref.dtype)


def _swa_attention(qkv, sinks, *, n_q_groups):
    b, s, _ = qkv.shape
    blk = SW_BLOCK
    d_q = n_q_groups * LANES
    kv_w = SW_KV_HEADS * LANES
    k_col = d_q // kv_w
    v_col = k_col + 1
    cur = lambda col: (lambda bi, n: (bi, n, col))
    prev = lambda col: (lambda bi, n: (bi, jnp.maximum(n - 1, 0), col))
    return pl.pallas_call(
        _swa_attn_kernel,
        grid=(b, s // blk),
        in_specs=[
            pl.BlockSpec((None, blk, d_q), cur(0)),
            pl.BlockSpec((None, blk, kv_w), cur(k_col)),
            pl.BlockSpec((None, blk, kv_w), prev(k_col)),
            pl.BlockSpec((None, blk, kv_w), cur(v_col)),
            pl.BlockSpec((None, blk, kv_w), prev(v_col)),
            _resident((SW_KV_HEADS, SW_GROUP, 1, 1)),
        ],
        out_specs=pl.BlockSpec((None, blk, d_q), cur(0)),
        out_shape=jax.ShapeDtypeStruct((b, s, d_q), BF16),
        compiler_params=_params(2),
        name="swa_attention",
    )(qkv, qkv, qkv, qkv, qkv, sinks)


def _dup_heads(w, n_heads):
    d = w.shape[0]
    w = w.reshape(d, n_heads, 1, HEAD_DIM)
    return jnp.broadcast_to(w, (d, n_heads, LANES // HEAD_DIM, HEAD_DIM)).reshape(d, n_heads * LANES)


def kernel(x, positions, norm_gains, ffn_w_gate_up, ffn_w_down, sb_w_in, sb_w_out,
           sw_w_in, sw_w_out, sw_q_gain, sw_k_gain, sw_sinks):
    b, s, d = x.shape
    m = b * s
    depth = norm_gains.shape[0]
    sb_heads = sb_w_out.shape[1] // HEAD_DIM
    d_q = sw_w_out.shape[1]
    n_q_groups = d_q // LANES
    d_kv = SW_KV_HEADS * HEAD_DIM

    cos_t, sin_up, sin_dn = _rope_tables(positions)
    idx = jnp.arange(SB_BLOCK)
    tri = jnp.where(idx[:, None] >= idx[None, :], -1.0, 0.0).astype(BF16)
    tri2 = jnp.concatenate([tri, tri], axis=0)
    lane_head = jnp.arange(LANES) // HEAD_DIM
    seg = (lane_head[:, None] == lane_head[None, :]).astype(BF16)
    seg2 = jnp.concatenate([seg, seg], axis=0)

    h = x.reshape(m, d)
    for i in range(depth):
        slot = i // 2
        gains = norm_gains[i].reshape(3, 1, d)
        h = _ffn(h, gains[0], ffn_w_gate_up[i, 0].astype(BF16), ffn_w_down[i, 0].astype(BF16))
        if i % 2 == 0:
            qkv = _norm_proj(h, gains[1], sb_w_in[slot].astype(BF16),
                             n_scaled=sb_heads * HEAD_DIM, scale=HEAD_DIM ** -0.5)
            o = _sb_attention(qkv.reshape(b, s, -1), tri2, n_heads=sb_heads)
            h = _proj_residual(o.reshape(m, -1), sb_w_out[slot].astype(BF16), h)
        else:
            w_in = sw_w_in[slot]
            w_ext = jnp.concatenate(
                [w_in[:, :d_q],
                 _dup_heads(w_in[:, d_q:d_q + d_kv], SW_KV_HEADS),
                 _dup_heads(w_in[:, d_q + d_kv:], SW_KV_HEADS)], axis=1).astype(BF16)
            reps = LANES // HEAD_DIM
            head_gains = jnp.stack([jnp.tile(sw_q_gain[slot], reps), jnp.tile(sw_k_gain[slot], reps)])
            qkv = _sw_proj(h, gains[1], w_ext, cos_t, sin_up, sin_dn, head_gains, seg2,
                           n_q_groups=n_q_groups, n_k_groups=SW_KV_HEADS)
            sinks = sw_sinks[slot].reshape(SW_KV_HEADS, SW_GROUP, 1, 1)
            o = _swa_attention(qkv.reshape(b, s, -1), sinks, n_q_groups=n_q_groups)
            h = _proj_residual(o.reshape(m, -1), sw_w_out[slot].astype(BF16), h)
        h = _ffn(h, gains[2], ffn_w_gate_up[i, 1].astype(BF16), ffn_w_down[i, 1].astype(BF16))
    return h.reshape(b, s, d)
```

```python
import functools

import jax
import jax.numpy as jnp
from jax import lax
from jax.experimental import pallas as pl
from jax.experimental.pallas import tpu as pltpu

F32 = jnp.float32
BF16 = jnp.bfloat16

HEAD_DIM = 64
LANES = 128
SW_KV_HEADS = 2
SW_GROUP = 8
SW_BLOCK = 128
SB_BLOCK = 256
SB_GROUP = 2
ROPE_THETA = 500000.0
ROT_DIM = HEAD_DIM // 4
EPS = 1e-6
LOG2_E = 1.4426950408889634
VMEM_LIMIT_BYTES = 56 * 1024 * 1024
ROW_TILE = 512
FF_TILE = 256


def _params(n_axes):
    return pltpu.CompilerParams(
        dimension_semantics=("arbitrary",) * n_axes,
        vmem_limit_bytes=VMEM_LIMIT_BYTES)


def _resident(shape):
    zeros = (0,) * len(shape)
    return pl.BlockSpec(shape, lambda *_: zeros, pipeline_mode=pl.Buffered(1))


def _rms_norm_rows(x, gain):
    ms = jnp.mean(x * x, axis=-1, keepdims=True)
    return x * lax.rsqrt(ms + EPS) * gain


def _dot(a, b):
    return jnp.dot(a, b, preferred_element_type=F32)


def _dot_nt(a, b):
    return lax.dot_general(a, b, (((1,), (1,)), ((), ())), preferred_element_type=F32)


def _neg_abs(x):
    bits = lax.bitcast_convert_type(x, jnp.uint32) | jnp.uint32(0x80000000)
    return lax.bitcast_convert_type(bits, F32)


def _split_bf16(x):
    hi = x.astype(BF16)
    lo = (x - hi.astype(F32)).astype(BF16)
    return hi, lo


def _ffn_kernel(x_ref, g_ref, wgu_ref, wd_ref, o_ref, *, d_ff):
    x = x_ref[...]
    h = _rms_norm_rows(x, g_ref[...]).astype(BF16)
    y = jnp.zeros_like(x)
    for c in range(d_ff // FF_TILE):
        lo = c * FF_TILE
        gate = _dot(h, wgu_ref[:, lo:lo + FF_TILE])
        up = _dot(h, wgu_ref[:, d_ff + lo:d_ff + lo + FF_TILE])
        act = gate * (1.0 / (1.0 + jnp.exp(-gate))) * up
        y = y + _dot(act.astype(BF16), wd_ref[lo:lo + FF_TILE, :])
    o_ref[...] = x + 0.5 * y


def _ffn(x, gain, w_gate_up, w_down):
    m, d = x.shape
    d_ff = w_down.shape[0]
    return pl.pallas_call(
        functools.partial(_ffn_kernel, d_ff=d_ff),
        grid=(m // ROW_TILE,),
        in_specs=[
            pl.BlockSpec((ROW_TILE, d), lambda i: (i, 0)),
            _resident((1, d)),
            _resident((d, 2 * d_ff)),
            _resident((d_ff, d)),
        ],
        out_specs=pl.BlockSpec((ROW_TILE, d), lambda i: (i, 0)),
        out_shape=jax.ShapeDtypeStruct((m, d), F32),
        compiler_params=_params(1),
        name="ffn",
    )(x, gain, w_gate_up, w_down)


def _norm_proj_kernel(x_ref, g_ref, w_ref, o_ref, *, n_scaled, scale):
    h = _rms_norm_rows(x_ref[...], g_ref[...]).astype(BF16)
    y = _dot(h, w_ref[...])
    o_ref[:, :n_scaled] = (y[:, :n_scaled] * scale).astype(o_ref.dtype)
    o_ref[:, n_scaled:] = y[:, n_scaled:].astype(o_ref.dtype)


def _norm_proj(x, gain, w, *, n_scaled, scale):
    m, d = x.shape
    n = w.shape[1]
    return pl.pallas_call(
        functools.partial(_norm_proj_kernel, n_scaled=n_scaled, scale=scale),
        grid=(m // ROW_TILE,),
        in_specs=[
            pl.BlockSpec((ROW_TILE, d), lambda i: (i, 0)),
            _resident((1, d)),
            _resident((d, n)),
        ],
        out_specs=pl.BlockSpec((ROW_TILE, n), lambda i: (i, 0)),
        out_shape=jax.ShapeDtypeStruct((m, n), BF16),
        compiler_params=_params(1),
        name="norm_proj",
    )(x, gain, w)


def _proj_residual_kernel(a_ref, w_ref, x_ref, o_ref):
    o_ref[...] = x_ref[...] + _dot(a_ref[...], w_ref[...])


def _proj_residual(a, w, x):
    m, d = x.shape
    k = a.shape[1]
    return pl.pallas_call(
        _proj_residual_kernel,
        grid=(m // ROW_TILE,),
        in_specs=[
            pl.BlockSpec((ROW_TILE, k), lambda i: (i, 0)),
            _resident((k, d)),
            pl.BlockSpec((ROW_TILE, d), lambda i: (i, 0)),
        ],
        out_specs=pl.BlockSpec((ROW_TILE, d), lambda i: (i, 0)),
        out_shape=jax.ShapeDtypeStruct((m, d), F32),
        compiler_params=_params(1),
        name="proj_residual",
    )(a, w, x)


def _sb_attn_kernel(tq_ref, tj_ref, ts_ref, q_ref, k_ref, v_ref, tri_ref, o_ref, acc_ref, c_ref,
                    *, n_blk, n_groups):
    blk = SB_BLOCK
    lane = lax.broadcasted_iota(jnp.int32, (blk, LANES), 1)
    first = lane < HEAD_DIM

    def tiles(qis, kjs, carries, causal):
        qs, kbs, vbs = [], [], []
        for i, j in zip(qis, kjs):
            q = q_ref[pl.ds(pl.multiple_of(i * blk, blk), blk), :]
            zero = jnp.zeros_like(q)
            qs.append(jnp.concatenate([jnp.where(first, q, zero), jnp.where(first, zero, q)], axis=0))
            start = pl.multiple_of(j * blk, blk)
            kbs.append(k_ref[pl.ds(start, blk), :])
            vbs.append(v_ref[pl.ds(start, blk), :])
        zs = [_dot_nt(q, kb) for q, kb in zip(qs, kbs)]
        sps = [jnp.maximum(z, 0.0) + jnp.log2(1.0 + jnp.exp2(_neg_abs(z))) for z in zs]
        if causal is not None:
            sps = [jnp.where(causal, sp, 0.0) for sp in sps]
        laters = [_dot(sp.astype(BF16), tri_ref[...]) for sp in sps]
        args = [z + later for z, later in zip(zs, laters)]
        if carries is not None:
            args = [arg + jnp.concatenate([carry] * (blk // LANES), axis=1)
                    for arg, carry in zip(args, carries)]
        ws = [jnp.exp2(arg) for arg in args]
        if causal is not None:
            ws = [jnp.where(causal, w, 0.0) for w in ws]
        pvs = [_dot(w.astype(BF16), vb) for w, vb in zip(ws, vbs)]
        return pvs, [jnp.broadcast_to(later[:, 0:1], (2 * blk, LANES)) for later in laters]

    row = lax.broadcasted_iota(jnp.int32, (2 * blk, blk), 0)
    row = jnp.where(row >= blk, row - blk, row)
    col = lax.broadcasted_iota(jnp.int32, (2 * blk, blk), 1)
    causal = col < row

    def diag_body(it, _):
        idx = [it * SB_GROUP + u for u in range(SB_GROUP)]
        pvs, totals = tiles(idx, idx, None, causal)
        for i, pv, total in zip(idx, pvs, totals):
            acc_ref[i] = pv
            c_ref[i] = total
        return 0

    lax.fori_loop(0, n_blk // SB_GROUP, diag_body, 0)
    acc_ref[n_blk] = jnp.zeros(acc_ref.shape[1:], F32)
    c_ref[n_blk] = jnp.zeros(c_ref.shape[1:], F32)

    def off_body(it, _):
        ts = [it * SB_GROUP + u for u in range(SB_GROUP)]
        slots = [ts_ref[t] for t in ts]
        carries = [c_ref[slot] for slot in slots]
        pvs, totals = tiles([tq_ref[t] for t in ts], [tj_ref[t] for t in ts], carries, None)
        for slot, carry, pv, total in zip(slots, carries, pvs, totals):
            acc_ref[slot] += pv
            c_ref[slot] = carry + total
        return 0

    lax.fori_loop(0, n_groups, off_body, 0)

    acc = acc_ref[0:n_blk]
    o = jnp.where(first[None], acc[:, :blk], acc[:, blk:])
    o_ref[...] = o.reshape(n_blk * blk, LANES).astype(o_ref.dtype)


def _sb_tile_order(n_blk):
    dummy = (0, 0, n_blk)
    order = []
    for d in range(1, n_blk):
        diag = [(i, i - d, i) for i in range(d, n_blk)]
        room = -len(order) % SB_GROUP
        if any(i in {t[2] for t in order[len(order) - (SB_GROUP - room):]} for i, _, _ in diag[:room]):
            order += [dummy] * room
        order += diag
    order += [dummy] * (-len(order) % SB_GROUP)
    return order


def _sb_attention(qkv, tri, *, n_heads):
    b, s, _ = qkv.shape
    n_pairs = n_heads * HEAD_DIM // LANES
    blk = SB_BLOCK
    n_blk = s // blk
    assert n_blk % SB_GROUP == 0
    order = _sb_tile_order(n_blk)
    tables = [jnp.array([t[c] for t in order], jnp.int32) for c in range(3)]
    seq = lambda col: pl.BlockSpec((None, s, LANES), lambda bi, h, *_: (bi, 0, col * n_pairs + h))
    grid_spec = pltpu.PrefetchScalarGridSpec(
        num_scalar_prefetch=3,
        grid=(b, n_pairs),
        in_specs=[
            seq(0), seq(1), seq(2),
            pl.BlockSpec((blk, blk), lambda bi, h, *_: (0, 0), pipeline_mode=pl.Buffered(1)),
        ],
        out_specs=pl.BlockSpec((None, s, LANES), lambda bi, h, *_: (bi, 0, h)),
        scratch_shapes=[
            pltpu.VMEM((n_blk + 1, 2 * blk, LANES), F32),
            pltpu.VMEM((n_blk + 1, 2 * blk, LANES), F32),
        ],
    )
    return pl.pallas_call(
        functools.partial(_sb_attn_kernel, n_blk=n_blk, n_groups=len(order) // SB_GROUP),
        grid_spec=grid_spec,
        out_shape=jax.ShapeDtypeStruct((b, s, n_pairs * LANES), BF16),
        compiler_params=_params(2),
        name="sb_attention",
    )(*tables, qkv, qkv, qkv, tri)


def _rope_table_kernel(pos_ref, invf_ref, cos_ref, sin_ref):
    ang = pos_ref[...].astype(F32) * invf_ref[...]
    cos_ref[...] = jnp.cos(ang)
    sin_ref[...] = jnp.sin(ang)


def _rope_tables(positions):
    b, s = positions.shape
    half = ROT_DIM // 2
    rows = s * half // LANES
    inv_freq = ROPE_THETA ** (-jnp.arange(0, ROT_DIM, 2, dtype=F32) / ROT_DIM)
    invf = jnp.tile(inv_freq, LANES // half).reshape(1, LANES)
    pos = jnp.repeat(positions, half, axis=1).reshape(b, rows, LANES)
    spec = pl.BlockSpec((None, rows, LANES), lambda i: (i, 0, 0))
    cos, sin = pl.pallas_call(
        _rope_table_kernel,
        grid=(b,),
        in_specs=[spec, _resident((1, LANES))],
        out_specs=[spec, spec],
        out_shape=[jax.ShapeDtypeStruct((b, rows, LANES), F32)] * 2,
        compiler_params=_params(1),
        name="rope_tables",
    )(pos, invf)
    cos = cos.reshape(b, s, half)
    sin = sin.reshape(b, s, half)
    rest = HEAD_DIM - ROT_DIM
    ones = jnp.ones((b, s, rest), F32)
    zeros = jnp.zeros((b, s, rest + half), F32)
    reps = LANES // HEAD_DIM
    cos_t = jnp.tile(jnp.concatenate([cos, cos, ones], axis=-1), (1, 1, reps))
    sin_up = jnp.tile(jnp.concatenate([sin, zeros], axis=-1), (1, 1, reps))
    sin_dn = jnp.tile(jnp.concatenate([zeros[..., :half], sin, zeros[..., :rest]], axis=-1),
                      (1, 1, reps))
    m = b * s
    return cos_t.reshape(m, LANES), sin_up.reshape(m, LANES), sin_dn.reshape(m, LANES)


def _sw_proj_kernel(x_ref, g_ref, w_ref, cos_ref, sup_ref, sdn_ref, hg_ref, seg_ref, o_ref,
                    *, n_q_groups, n_k_groups, n_groups):
    h = _rms_norm_rows(x_ref[...], g_ref[...]).astype(BF16)
    y = _dot(h, w_ref[...])
    cos_t = cos_ref[...]
    sin_up = sup_ref[...]
    sin_dn = sdn_ref[...]
    seg2 = seg_ref[...]
    for grp in range(n_groups):
        yg = y[:, grp * LANES:(grp + 1) * LANES]
        if grp < n_q_groups + n_k_groups:
            is_q = grp < n_q_groups
            hi, lo = _split_bf16(yg * yg)
            ss = _dot(jnp.concatenate([hi, lo], axis=1), seg2)
            gain = hg_ref[0:1, :] if is_q else hg_ref[1:2, :]
            yn = yg * lax.rsqrt(ss * (1.0 / HEAD_DIM) + EPS) * gain
            up = pltpu.roll(yn, LANES - ROT_DIM // 2, 1)
            dn = pltpu.roll(yn, ROT_DIM // 2, 1)
            r = yn * cos_t - up * sin_up + dn * sin_dn
            if is_q:
                r = r * (HEAD_DIM ** -0.5)
            o_ref[:, grp * LANES:(grp + 1) * LANES] = r.astype(o_ref.dtype)
        else:
            o_ref[:, grp * LANES:(grp + 1) * LANES] = yg.astype(o_ref.dtype)


def _sw_proj(x, gain, w, cos_t, sin_up, sin_dn, head_gains, seg2, *, n_q_groups, n_k_groups):
    m, d = x.shape
    n = w.shape[1]
    row = lambda i: (i, 0)
    return pl.pallas_call(
        functools.partial(_sw_proj_kernel, n_q_groups=n_q_groups, n_k_groups=n_k_groups,
                          n_groups=n // LANES),
        grid=(m // ROW_TILE,),
        in_specs=[
            pl.BlockSpec((ROW_TILE, d), row),
            _resident((1, d)),
            _resident((d, n)),
            pl.BlockSpec((ROW_TILE, LANES), row),
            pl.BlockSpec((ROW_TILE, LANES), row),
            pl.BlockSpec((ROW_TILE, LANES), row),
            _resident((2, LANES)),
            _resident((2 * LANES, LANES)),
        ],
        out_specs=pl.BlockSpec((ROW_TILE, n), row),
        out_shape=jax.ShapeDtypeStruct((m, n), BF16),
        compiler_params=_params(1),
        name="sw_proj",
    )(x, gain, w, cos_t, sin_up, sin_dn, head_gains, seg2)


def _swa_attn_kernel(q_ref, kc_ref, kp_ref, vc_ref, vp_ref, sink_ref, o_ref):
    blk = SW_BLOCK
    n = pl.program_id(1)
    lane = lax.broadcasted_iota(jnp.int32, (blk, LANES), 1)
    first = lane < HEAD_DIM
    t = lax.broadcasted_iota(jnp.int32, (blk, 2 * blk), 0)
    j = lax.broadcasted_iota(jnp.int32, (blk, 2 * blk), 1)
    mask = (j > t) & (j <= t + blk) & ((j >= blk) | (n > 0))
    ones = jnp.ones((2 * blk, LANES), BF16)
    heads = range(SW_KV_HEADS)
    kbs, vbs, lhs = [], [], []
    for kh in heads:
        cols = slice(kh * LANES, (kh + 1) * LANES)
        kbs.append(jnp.concatenate([kp_ref[:, cols], kc_ref[:, cols]], axis=0))
        vbs.append(jnp.concatenate([vp_ref[:, cols], vc_ref[:, cols]], axis=0))
        rows = []
        for g in range(SW_GROUP):
            pair = (kh * SW_GROUP + g) // 2
            qp = q_ref[:, pair * LANES:(pair + 1) * LANES]
            zero = jnp.zeros_like(qp)
            rows.append(jnp.where(first, qp, zero) if g % 2 == 0 else jnp.where(first, zero, qp))
        lhs.append(jnp.concatenate(rows, axis=0))
    ss = [_dot_nt(l, kb).reshape(SW_GROUP, blk, 2 * blk) for l, kb in zip(lhs, kbs)]
    ss = [jnp.where(mask[None], s, -jnp.inf) for s in ss]
    sinks = [sink_ref[kh] for kh in heads]
    ms = [jnp.maximum(jnp.max(s, axis=-1, keepdims=True), sink) for s, sink in zip(ss, sinks)]
    ms = [jnp.broadcast_to(m, (SW_GROUP, blk, LANES)) for m in ms]
    ps = [jnp.exp(s - jnp.concatenate([m, m], axis=-1)).astype(BF16).reshape(SW_GROUP * blk, 2 * blk)
          for s, m in zip(ss, ms)]
    denoms = [_dot(p, ones) + jnp.exp(sink - m).reshape(SW_GROUP * blk, LANES)
              for p, sink, m in zip(ps, sinks, ms)]
    outs = [_dot(p, vb) / denom for p, vb, denom in zip(ps, vbs, denoms)]
    for kh, o in zip(heads, outs):
        for g2 in range(SW_GROUP // 2):
            pair = kh * (SW_GROUP // 2) + g2
            even = o[(2 * g2) * blk:(2 * g2 + 1) * blk]
            odd = o[(2 * g2 + 1) * blk:(2 * g2 + 2) * blk]
            o_ref[:, pair * LANES:(pair + 1) * LANES] = jnp.where(first, even, odd).astype(o_ref.dtype)


def _swa_attention(qkv, sinks, *, n_q_groups):
    b, s, _ = qkv.shape
    blk = SW_BLOCK
    d_q = n_q_groups * LANES
    kv_w = SW_KV_HEADS * LANES
    k_col = d_q // kv_w
    v_col = k_col + 1
    cur = lambda col: (lambda bi, n: (bi, n, col))
    prev = lambda col: (lambda bi, n: (bi, jnp.maximum(n - 1, 0), col))
    return pl.pallas_call(
        _swa_attn_kernel,
        grid=(b, s // blk),
        in_specs=[
            pl.BlockSpec((None, blk, d_q), cur(0)),
            pl.BlockSpec((None, blk, kv_w), cur(k_col)),
            pl.BlockSpec((None, blk, kv_w), prev(k_col)),
            pl.BlockSpec((None, blk, kv_w), cur(v_col)),
            pl.BlockSpec((None, blk, kv_w), prev(v_col)),
            _resident((SW_KV_HEADS, SW_GROUP, 1, 1)),
        ],
        out_specs=pl.BlockSpec((None, blk, d_q), cur(0)),
        out_shape=jax.ShapeDtypeStruct((b, s, d_q), BF16),
        compiler_params=_params(2),
        name="swa_attention",
    )(qkv, qkv, qkv, qkv, qkv, sinks)


def _dup_heads(w, n_heads):
    d = w.shape[0]
    w = w.reshape(d, n_heads, 1, HEAD_DIM)
    return jnp.broadcast_to(w, (d, n_heads, LANES // HEAD_DIM, HEAD_DIM)).reshape(d, n_heads * LANES)


def kernel(x, positions, norm_gains, ffn_w_gate_up, ffn_w_down, sb_w_in, sb_w_out,
           sw_w_in, sw_w_out, sw_q_gain, sw_k_gain, sw_sinks):
    b, s, d = x.shape
    m = b * s
    depth = norm_gains.shape[0]
    sb_heads = sb_w_out.shape[1] // HEAD_DIM
    d_q = sw_w_out.shape[1]
    n_q_groups = d_q // LANES
    d_kv = SW_KV_HEADS * HEAD_DIM

    cos_t, sin_up, sin_dn = _rope_tables(positions)
    idx = jnp.arange(SB_BLOCK)
    tri = jnp.where(idx[:, None] >= idx[None, :], -1.0, 0.0).astype(BF16)
    lane_head = jnp.arange(LANES) // HEAD_DIM
    seg = (lane_head[:, None] == lane_head[None, :]).astype(BF16)
    seg2 = jnp.concatenate([seg, seg], axis=0)

    h = x.reshape(m, d)
    for i in range(depth):
        slot = i // 2
        gains = norm_gains[i].reshape(3, 1, d)
        h = _ffn(h, gains[0], ffn_w_gate_up[i, 0].astype(BF16), ffn_w_down[i, 0].astype(BF16))
        if i % 2 == 0:
            qkv = _norm_proj(h, gains[1], sb_w_in[slot].astype(BF16),
                             n_scaled=sb_heads * HEAD_DIM, scale=HEAD_DIM ** -0.5 * LOG2_E)
            o = _sb_attention(qkv.reshape(b, s, -1), tri, n_heads=sb_heads)
            h = _proj_residual(o.reshape(m, -1), sb_w_out[slot].astype(BF16), h)
        else:
            w_in = sw_w_in[slot]
            w_ext = jnp.concatenate(
                [w_in[:, :d_q],
                 _dup_heads(w_in[:, d_q:d_q + d_kv], SW_KV_HEADS),
                 _dup_heads(w_in[:, d_q + d_kv:], SW_KV_HEADS)], axis=1).astype(BF16)
            reps = LANES // HEAD_DIM
            head_gains = jnp.stack([jnp.tile(sw_q_gain[slot], reps), jnp.tile(sw_k_gain[slot], reps)])
            qkv = _sw_proj(h, gains[1], w_ext, cos_t, sin_up, sin_dn, head_gains, seg2,
                           n_q_groups=n_q_groups, n_k_groups=SW_KV_HEADS)
            sinks = sw_sinks[slot].reshape(SW_KV_HEADS, SW_GROUP, 1, 1)
            o = _swa_attention(qkv.reshape(b, s, -1), sinks, n_q_groups=n_q_groups)
            h = _proj_residual(o.reshape(m, -1), sw_w_out[slot].astype(BF16), h)
        h = _ffn(h, gains[2], ffn_w_gate_up[i, 1].astype(BF16), ffn_w_down[i, 1].astype(BF16))
    return h.reshape(b, s, d)
```

```python
import functools

import jax
import jax.numpy as jnp
from jax import lax
from jax.experimental import pallas as pl
from jax.experimental.pallas import tpu as pltpu

F32 = jnp.float32
BF16 = jnp.bfloat16

HEAD_DIM = 64
LANES = 128
SW_KV_HEADS = 2
SW_GROUP = 8
SW_BLOCK = 128
SB_BLOCK = 256
SB_DIAG_GROUP = 4
SB_GROUP = 1
SB_STEPS = 8
ROPE_THETA = 500000.0
ROT_DIM = HEAD_DIM // 4
EPS = 1e-6
LOG2_E = 1.4426950408889634
VMEM_LIMIT_BYTES = 56 * 1024 * 1024
ROW_TILE = 512
FF_TILE = 256


def _params(n_axes):
    return pltpu.CompilerParams(
        dimension_semantics=("arbitrary",) * n_axes,
        vmem_limit_bytes=VMEM_LIMIT_BYTES)


def _resident(shape):
    zeros = (0,) * len(shape)
    return pl.BlockSpec(shape, lambda *_: zeros, pipeline_mode=pl.Buffered(1))


def _rms_norm_rows(x, gain):
    ms = jnp.mean(x * x, axis=-1, keepdims=True)
    return x * lax.rsqrt(ms + EPS) * gain


def _dot(a, b):
    return jnp.dot(a, b, preferred_element_type=F32)


def _dot_nt(a, b):
    return lax.dot_general(a, b, (((1,), (1,)), ((), ())), preferred_element_type=F32)


def _neg_abs(x):
    bits = lax.bitcast_convert_type(x, jnp.uint32) | jnp.uint32(0x80000000)
    return lax.bitcast_convert_type(bits, F32)


def _split_bf16(x):
    hi = x.astype(BF16)
    lo = (x - hi.astype(F32)).astype(BF16)
    return hi, lo


def _ffn_kernel(*refs, d_ff, has_attn):
    if has_attn:
        a_ref, wo_ref, x_ref, g_ref, wgu_ref, wd_ref, o_ref = refs
        x = x_ref[...] + _dot(a_ref[...], wo_ref[...])
    else:
        x_ref, g_ref, wgu_ref, wd_ref, o_ref = refs
        x = x_ref[...]
    h = _rms_norm_rows(x, g_ref[...]).astype(BF16)
    y = jnp.zeros_like(x)
    for c in range(d_ff // FF_TILE):
        lo = c * FF_TILE
        gate = _dot(h, wgu_ref[:, lo:lo + FF_TILE])
        up = _dot(h, wgu_ref[:, d_ff + lo:d_ff + lo + FF_TILE])
        act = gate * (1.0 / (1.0 + jnp.exp(-gate))) * up
        y = y + _dot(act.astype(BF16), wd_ref[lo:lo + FF_TILE, :])
    o_ref[...] = x + 0.5 * y


def _layer_weights(shape, *index):
    lead = len(index)
    return pl.BlockSpec((None,) * lead + tuple(shape), lambda *_: tuple(index) + (0, 0),
                        pipeline_mode=pl.Buffered(1))


def _ffn(x, gain, w_gate_up, w_down, layer, which, attn=None, w_out=None, slot=None):
    m, d = x.shape
    d_ff = w_down.shape[-2]
    row = lambda i: (i, 0)
    in_specs, args = [], []
    if attn is not None:
        k = attn.shape[1]
        in_specs += [pl.BlockSpec((ROW_TILE, k), row), _layer_weights((k, d), slot)]
        args += [attn, w_out]
    in_specs += [
        pl.BlockSpec((ROW_TILE, d), row),
        _resident((1, d)),
        _layer_weights((d, 2 * d_ff), layer, which),
        _layer_weights((d_ff, d), layer, which),
    ]
    args += [x, gain, w_gate_up, w_down]
    return pl.pallas_call(
        functools.partial(_ffn_kernel, d_ff=d_ff, has_attn=attn is not None),
        grid=(m // ROW_TILE,),
        in_specs=in_specs,
        out_specs=pl.BlockSpec((ROW_TILE, d), row),
        out_shape=jax.ShapeDtypeStruct((m, d), F32),
        compiler_params=_params(1),
        name="ffn",
    )(*args)


def _norm_proj_kernel(x_ref, g_ref, w_ref, o_ref, *, n_scaled, scale):
    h = _rms_norm_rows(x_ref[...], g_ref[...]).astype(BF16)
    y = _dot(h, w_ref[...])
    o_ref[:, :n_scaled] = (y[:, :n_scaled] * scale).astype(o_ref.dtype)
    o_ref[:, n_scaled:] = y[:, n_scaled:].astype(o_ref.dtype)


def _norm_proj(x, gain, w, *, n_scaled, scale):
    m, d = x.shape
    n = w.shape[1]
    return pl.pallas_call(
        functools.partial(_norm_proj_kernel, n_scaled=n_scaled, scale=scale),
        grid=(m // ROW_TILE,),
        in_specs=[
            pl.BlockSpec((ROW_TILE, d), lambda i: (i, 0)),
            _resident((1, d)),
            _resident((d, n)),
        ],
        out_specs=pl.BlockSpec((ROW_TILE, n), lambda i: (i, 0)),
        out_shape=jax.ShapeDtypeStruct((m, n), BF16),
        compiler_params=_params(1),
        name="norm_proj",
    )(x, gain, w)


def _sb_attn_kernel(tq_ref, tj_ref, ts_ref, q_ref, k_ref, v_ref, tri_ref, o_ref,
                    acc_ref, c_ref, qs_ref, z0_ref, z1_ref, a0_ref, a1_ref, *, n_blk, n_groups):
    blk = SB_BLOCK
    lane = lax.broadcasted_iota(jnp.int32, (blk, LANES), 1)
    first = lane < HEAD_DIM

    def logits(qis, kjs):
        return [_dot_nt(qs_ref[i], k_ref[pl.ds(pl.multiple_of(j * blk, blk), blk), :])
                for i, j in zip(qis, kjs)]

    def log_weights(zs, carries, causal):
        sps = [jnp.maximum(z, 0.0) + jnp.log2(1.0 + jnp.exp2(_neg_abs(z))) for z in zs]
        if causal is not None:
            sps = [jnp.where(causal, sp, 0.0) for sp in sps]
        laters = [_dot(sp.astype(BF16), tri_ref[...]) for sp in sps]
        args = [z + later for z, later in zip(zs, laters)]
        if carries is not None:
            args = [arg + jnp.concatenate([carry] * (blk // LANES), axis=1)
                    for arg, carry in zip(args, carries)]
        if causal is not None:
            args = [jnp.where(causal, arg, -jnp.inf) for arg in args]
        return args, [jnp.broadcast_to(later[:, 0:1], (2 * blk, LANES)) for later in laters]

    def values(args, kjs):
        vbs = [v_ref[pl.ds(pl.multiple_of(j * blk, blk), blk), :] for j in kjs]
        return [_dot(jnp.exp2(arg).astype(BF16), vb) for arg, vb in zip(args, vbs)]

    q = q_ref[...].reshape(n_blk, blk, LANES)
    zero = jnp.zeros_like(q)
    qs_ref[:, 0:blk, :] = jnp.where(first[None], q, zero)
    qs_ref[:, blk:2 * blk, :] = jnp.where(first[None], zero, q)

    row = lax.broadcasted_iota(jnp.int32, (2 * blk, blk), 0)
    row = jnp.where(row >= blk, row - blk, row)
    col = lax.broadcasted_iota(jnp.int32, (2 * blk, blk), 1)
    causal = col < row

    def diag_body(it, _):
        idx = [it * SB_DIAG_GROUP + u for u in range(SB_DIAG_GROUP)]
        args, totals = log_weights(logits(idx, idx), None, causal)
        pvs = values(args, idx)
        for i, pv, total in zip(idx, pvs, totals):
            acc_ref[i] = pv
            c_ref[i] = total
        return 0

    lax.fori_loop(0, n_blk // SB_DIAG_GROUP, diag_body, 0)
    acc_ref[n_blk] = jnp.zeros(acc_ref.shape[1:], F32)
    c_ref[n_blk] = jnp.zeros(c_ref.shape[1:], F32)

    def group(g):
        return [g * SB_GROUP + u for u in range(SB_GROUP)]

    def logit_stage(g, zw_ref):
        ts = group(g)
        for u, z in enumerate(logits([tq_ref[t] for t in ts], [tj_ref[t] for t in ts])):
            zw_ref[u] = z

    def weight_stage(g, zr_ref, aw_ref):
        slots = [ts_ref[t] for t in group(g)]
        carries = [c_ref[slot] for slot in slots]
        args, totals = log_weights([zr_ref[u] for u in range(SB_GROUP)], carries, None)
        for u, (slot, carry, arg, total) in enumerate(zip(slots, carries, args, totals)):
            aw_ref[u] = arg
            c_ref[slot] = carry + total

    def value_stage(g, ar_ref):
        ts = group(g)
        pvs = values([ar_ref[u] for u in range(SB_GROUP)], [tj_ref[t] for t in ts])
        for t, pv in zip(ts, pvs):
            acc_ref[ts_ref[t]] += pv

    z_refs = (z0_ref, z1_ref)
    a_refs = (a0_ref, a1_ref)

    def step(g, parity):
        logit_stage(g, z_refs[parity])
        weight_stage(g - 1, z_refs[1 - parity], a_refs[1 - parity])
        value_stage(g - 2, a_refs[parity])

    logit_stage(0, z_refs[0])
    logit_stage(1, z_refs[1])
    weight_stage(0, z_refs[0], a_refs[0])
    n_bodies = (n_groups - 2) // SB_STEPS

    def off_body(h, _):
        for k in range(SB_STEPS):
            step(2 + SB_STEPS * h + k, k % 2)
        return 0

    lax.fori_loop(0, n_bodies, off_body, 0)
    for g in range(2 + SB_STEPS * n_bodies, n_groups):
        step(g, g % 2)
    last = (n_groups - 1) % 2
    weight_stage(n_groups - 1, z_refs[last], a_refs[last])
    value_stage(n_groups - 2, a_refs[1 - last])
    value_stage(n_groups - 1, a_refs[last])

    acc = acc_ref[0:n_blk]
    o = jnp.where(first[None], acc[:, :blk], acc[:, blk:])
    o_ref[...] = o.reshape(n_blk * blk, LANES).astype(o_ref.dtype)


def _sb_tile_order(n_blk):
    dummy = (0, 0, n_blk)
    order = []
    for d in range(1, n_blk):
        diag = [(i, i - d, i) for i in range(d, n_blk)]
        room = -len(order) % SB_GROUP
        if any(i in {t[2] for t in order[len(order) - (SB_GROUP - room):]} for i, _, _ in diag[:room]):
            order += [dummy] * room
        order += diag
    order += [dummy] * (-len(order) % SB_GROUP)
    while len(order) // SB_GROUP < 2 or (len(order) // SB_GROUP - 2) % SB_STEPS:
        order += [dummy] * SB_GROUP
    return order


def _sb_attention(qkv, tri, *, n_heads):
    b, s, _ = qkv.shape
    n_pairs = n_heads * HEAD_DIM // LANES
    blk = SB_BLOCK
    n_blk = s // blk
    assert n_blk % SB_DIAG_GROUP == 0 and SB_STEPS % 2 == 0
    order = _sb_tile_order(n_blk)
    tables = [jnp.array([t[c] for t in order], jnp.int32) for c in range(3)]
    seq = lambda col: pl.BlockSpec((None, s, LANES), lambda bi, h, *_: (bi, 0, col * n_pairs + h))
    grid_spec = pltpu.PrefetchScalarGridSpec(
        num_scalar_prefetch=3,
        grid=(b, n_pairs),
        in_specs=[
            seq(0), seq(1), seq(2),
            pl.BlockSpec((blk, blk), lambda bi, h, *_: (0, 0), pipeline_mode=pl.Buffered(1)),
        ],
        out_specs=pl.BlockSpec((None, s, LANES), lambda bi, h, *_: (bi, 0, h)),
        scratch_shapes=[
            pltpu.VMEM((n_blk + 1, 2 * blk, LANES), F32),
            pltpu.VMEM((n_blk + 1, 2 * blk, LANES), F32),
            pltpu.VMEM((n_blk, 2 * blk, LANES), BF16),
        ] + [pltpu.VMEM((SB_GROUP, 2 * blk, blk), F32)] * 4,
    )
    return pl.pallas_call(
        functools.partial(_sb_attn_kernel, n_blk=n_blk, n_groups=len(order) // SB_GROUP),
        grid_spec=grid_spec,
        out_shape=jax.ShapeDtypeStruct((b, s, n_pairs * LANES), BF16),
        compiler_params=_params(2),
        name="sb_attention",
    )(*tables, qkv, qkv, qkv, tri)


def _rope_table_kernel(pos_ref, invf_ref, cos_ref, sin_ref):
    ang = pos_ref[...].astype(F32) * invf_ref[...]
    cos_ref[...] = jnp.cos(ang)
    sin_ref[...] = jnp.sin(ang)


def _rope_tables(positions):
    b, s = positions.shape
    half = ROT_DIM // 2
    rows = s * half // LANES
    inv_freq = ROPE_THETA ** (-jnp.arange(0, ROT_DIM, 2, dtype=F32) / ROT_DIM)
    invf = jnp.tile(inv_freq, LANES // half).reshape(1, LANES)
    pos = jnp.repeat(positions, half, axis=1).reshape(b, rows, LANES)
    spec = pl.BlockSpec((None, rows, LANES), lambda i: (i, 0, 0))
    cos, sin = pl.pallas_call(
        _rope_table_kernel,
        grid=(b,),
        in_specs=[spec, _resident((1, LANES))],
        out_specs=[spec, spec],
        out_shape=[jax.ShapeDtypeStruct((b, rows, LANES), F32)] * 2,
        compiler_params=_params(1),
        name="rope_tables",
    )(pos, invf)
    cos = cos.reshape(b, s, half)
    sin = sin.reshape(b, s, half)
    rest = HEAD_DIM - ROT_DIM
    ones = jnp.ones((b, s, rest), F32)
    zeros = jnp.zeros((b, s, rest + half), F32)
    reps = LANES // HEAD_DIM
    cos_t = jnp.tile(jnp.concatenate([cos, cos, ones], axis=-1), (1, 1, reps))
    sin_up = jnp.tile(jnp.concatenate([sin, zeros], axis=-1), (1, 1, reps))
    sin_dn = jnp.tile(jnp.concatenate([zeros[..., :half], sin, zeros[..., :rest]], axis=-1),
                      (1, 1, reps))
    m = b * s
    return cos_t.reshape(m, LANES), sin_up.reshape(m, LANES), sin_dn.reshape(m, LANES)


def _sw_proj_kernel(x_ref, g_ref, w_ref, cos_ref, sup_ref, sdn_ref, hg_ref, seg_ref, o_ref,
                    *, n_q_groups, n_k_groups, n_groups):
    h = _rms_norm_rows(x_ref[...], g_ref[...]).astype(BF16)
    y = _dot(h, w_ref[...])
    cos_t = cos_ref[...]
    sin_up = sup_ref[...]
    sin_dn = sdn_ref[...]
    seg2 = seg_ref[...]
    for grp in range(n_groups):
        yg = y[:, grp * LANES:(grp + 1) * LANES]
        if grp < n_q_groups + n_k_groups:
            is_q = grp < n_q_groups
            hi, lo = _split_bf16(yg * yg)
            ss = _dot(jnp.concatenate([hi, lo], axis=1), seg2)
            gain = hg_ref[0:1, :] if is_q else hg_ref[1:2, :]
            yn = yg * lax.rsqrt(ss * (1.0 / HEAD_DIM) + EPS) * gain
            up = pltpu.roll(yn, LANES - ROT_DIM // 2, 1)
            dn = pltpu.roll(yn, ROT_DIM // 2, 1)
            r = yn * cos_t - up * sin_up + dn * sin_dn
            if is_q:
                r = r * (HEAD_DIM ** -0.5)
            o_ref[:, grp * LANES:(grp + 1) * LANES] = r.astype(o_ref.dtype)
        else:
            o_ref[:, grp * LANES:(grp + 1) * LANES] = yg.astype(o_ref.dtype)


def _sw_proj(x, gain, w, cos_t, sin_up, sin_dn, head_gains, seg2, *, n_q_groups, n_k_groups):
    m, d = x.shape
    n = w.shape[1]
    row = lambda i: (i, 0)
    return pl.pallas_call(
        functools.partial(_sw_proj_kernel, n_q_groups=n_q_groups, n_k_groups=n_k_groups,
                          n_groups=n // LANES),
        grid=(m // ROW_TILE,),
        in_specs=[
            pl.BlockSpec((ROW_TILE, d), row),
            _resident((1, d)),
            _resident((d, n)),
            pl.BlockSpec((ROW_TILE, LANES), row),
            pl.BlockSpec((ROW_TILE, LANES), row),
            pl.BlockSpec((ROW_TILE, LANES), row),
            _resident((2, LANES)),
            _resident((2 * LANES, LANES)),
        ],
        out_specs=pl.BlockSpec((ROW_TILE, n), row),
        out_shape=jax.ShapeDtypeStruct((m, n), BF16),
        compiler_params=_params(1),
        name="sw_proj",
    )(x, gain, w, cos_t, sin_up, sin_dn, head_gains, seg2)


def _swa_attn_kernel(q_ref, kc_ref, kp_ref, vc_ref, vp_ref, sink_ref, o_ref):
    blk = SW_BLOCK
    n = pl.program_id(1)
    lane = lax.broadcasted_iota(jnp.int32, (blk, LANES), 1)
    first = lane < HEAD_DIM
    t = lax.broadcasted_iota(jnp.int32, (blk, 2 * blk), 0)
    j = lax.broadcasted_iota(jnp.int32, (blk, 2 * blk), 1)
    mask = (j > t) & (j <= t + blk) & ((j >= blk) | (n > 0))
    ones = jnp.ones((2 * blk, LANES), BF16)
    heads = range(SW_KV_HEADS)
    kbs, vbs, lhs = [], [], []
    for kh in heads:
        cols = slice(kh * LANES, (kh + 1) * LANES)
        kbs.append(jnp.concatenate([kp_ref[:, cols], kc_ref[:, cols]], axis=0))
        vbs.append(jnp.concatenate([vp_ref[:, cols], vc_ref[:, cols]], axis=0))
        rows = []
        for g in range(SW_GROUP):
            pair = (kh * SW_GROUP + g) // 2
            qp = q_ref[:, pair * LANES:(pair + 1) * LANES]
            zero = jnp.zeros_like(qp)
            rows.append(jnp.where(first, qp, zero) if g % 2 == 0 else jnp.where(first, zero, qp))
        lhs.append(jnp.concatenate(rows, axis=0))
    ss = [_dot_nt(l, kb).reshape(SW_GROUP, blk, 2 * blk) for l, kb in zip(lhs, kbs)]
    ss = [jnp.where(mask[None], s, -jnp.inf) for s in ss]
    sinks = [sink_ref[kh] for kh in heads]
    ms = [jnp.maximum(jnp.max(s, axis=-1, keepdims=True), sink) for s, sink in zip(ss, sinks)]
    ms = [jnp.broadcast_to(m, (SW_GROUP, blk, LANES)) for m in ms]
    ps = [jnp.exp(s - jnp.concatenate([m, m], axis=-1)).astype(BF16).reshape(SW_GROUP * blk, 2 * blk)
          for s, m in zip(ss, ms)]
    denoms = [_dot(p, ones) + jnp.exp(sink - m).reshape(SW_GROUP * blk, LANES)
              for p, sink, m in zip(ps, sinks, ms)]
    outs = [_dot(p, vb) / denom for p, vb, denom in zip(ps, vbs, denoms)]
    for kh, o in zip(heads, outs):
        for g2 in range(SW_GROUP // 2):
            pair = kh * (SW_GROUP // 2) + g2
            even = o[(2 * g2) * blk:(2 * g2 + 1) * blk]
            odd = o[(2 * g2 + 1) * blk:(2 * g2 + 2) * blk]
            o_ref[:, pair * LANES:(pair + 1) * LANES] = jnp.where(first, even, odd).astype(o_ref.dtype)


def _swa_attention(qkv, sinks, *, n_q_groups):
    b, s, _ = qkv.shape
    blk = SW_BLOCK
    d_q = n_q_groups * LANES
    kv_w = SW_KV_HEADS * LANES
    k_col = d_q // kv_w
    v_col = k_col + 1
    cur = lambda col: (lambda bi, n: (bi, n, col))
    prev = lambda col: (lambda bi, n: (bi, jnp.maximum(n - 1, 0), col))
    return pl.pallas_call(
        _swa_attn_kernel,
        grid=(b, s // blk),
        in_specs=[
            pl.BlockSpec((None, blk, d_q), cur(0)),
            pl.BlockSpec((None, blk, kv_w), cur(k_col)),
            pl.BlockSpec((None, blk, kv_w), prev(k_col)),
            pl.BlockSpec((None, blk, kv_w), cur(v_col)),
            pl.BlockSpec((None, blk, kv_w), prev(v_col)),
            _resident((SW_KV_HEADS, SW_GROUP, 1, 1)),
        ],
        out_specs=pl.BlockSpec((None, blk, d_q), cur(0)),
        out_shape=jax.ShapeDtypeStruct((b, s, d_q), BF16),
        compiler_params=_params(2),
        name="swa_attention",
    )(qkv, qkv, qkv, qkv, qkv, sinks)


def _dup_heads(w, n_heads):
    d = w.shape[0]
    w = w.reshape(d, n_heads, 1, HEAD_DIM)
    return jnp.broadcast_to(w, (d, n_heads, LANES // HEAD_DIM, HEAD_DIM)).reshape(d, n_heads * LANES)


def kernel(x, positions, norm_gains, ffn_w_gate_up, ffn_w_down, sb_w_in, sb_w_out,
           sw_w_in, sw_w_out, sw_q_gain, sw_k_gain, sw_sinks):
    b, s, d = x.shape
    m = b * s
    depth = norm_gains.shape[0]
    sb_heads = sb_w_out.shape[1] // HEAD_DIM
    d_q = sw_w_out.shape[1]
    n_q_groups = d_q // LANES
    d_kv = SW_KV_HEADS * HEAD_DIM

    cos_t, sin_up, sin_dn = _rope_tables(positions)
    idx = jnp.arange(SB_BLOCK)
    tri = jnp.where(idx[:, None] >= idx[None, :], -1.0, 0.0).astype(BF16)
    lane_head = jnp.arange(LANES) // HEAD_DIM
    seg = (lane_head[:, None] == lane_head[None, :]).astype(BF16)
    seg2 = jnp.concatenate([seg, seg], axis=0)

    w_gate_up = ffn_w_gate_up.astype(BF16)
    w_down = ffn_w_down.astype(BF16)
    w_out = (sb_w_out.astype(BF16), sw_w_out.astype(BF16))

    h = x.reshape(m, d)
    for i in range(depth):
        slot = i // 2
        gains = norm_gains[i].reshape(3, 1, d)
        h = _ffn(h, gains[0], w_gate_up, w_down, i, 0)
        if i % 2 == 0:
            qkv = _norm_proj(h, gains[1], sb_w_in[slot].astype(BF16),
                             n_scaled=sb_heads * HEAD_DIM, scale=HEAD_DIM ** -0.5 * LOG2_E)
            o = _sb_attention(qkv.reshape(b, s, -1), tri, n_heads=sb_heads)
        else:
            w_in = sw_w_in[slot]
            w_ext = jnp.concatenate(
                [w_in[:, :d_q],
                 _dup_heads(w_in[:, d_q:d_q + d_kv], SW_KV_HEADS),
                 _dup_heads(w_in[:, d_q + d_kv:], SW_KV_HEADS)], axis=1).astype(BF16)
            reps = LANES // HEAD_DIM
            head_gains = jnp.stack([jnp.tile(sw_q_gain[slot], reps), jnp.tile(sw_k_gain[slot], reps)])
            qkv = _sw_proj(h, gains[1], w_ext, cos_t, sin_up, sin_dn, head_gains, seg2,
                           n_q_groups=n_q_groups, n_k_groups=SW_KV_HEADS)
            sinks = sw_sinks[slot].reshape(SW_KV_HEADS, SW_GROUP, 1, 1)
            o = _swa_attention(qkv.reshape(b, s, -1), sinks, n_q_groups=n_q_groups)
        h = _ffn(h, gains[2], w_gate_up, w_down, i, 1,
                 attn=o.reshape(m, -1), w_out=w_out[i % 2], slot=slot)
    return h.reshape(b, s, d)
```

```python
import functools

import jax
import jax.numpy as jnp
from jax import lax
from jax.experimental import pallas as pl
from jax.experimental.pallas import tpu as pltpu

F32 = jnp.float32
BF16 = jnp.bfloat16

HEAD_DIM = 64
LANES = 128
SW_KV_HEADS = 2
SW_GROUP = 8
SW_BLOCK = 128
SW_STEP_BLOCKS = 4
SB_BLOCK = 256
SB_DIAG_GROUP = 4
SB_GROUP = 1
SB_STEPS = 12
ROPE_THETA = 500000.0
ROT_DIM = HEAD_DIM // 4
EPS = 1e-6
LOG2_E = 1.4426950408889634
VMEM_LIMIT_BYTES = 56 * 1024 * 1024
ROW_TILE = 512
FF_TILE = 256


def _params(n_axes):
    return pltpu.CompilerParams(
        dimension_semantics=("arbitrary",) * n_axes,
        vmem_limit_bytes=VMEM_LIMIT_BYTES)


def _resident(shape):
    zeros = (0,) * len(shape)
    return pl.BlockSpec(shape, lambda *_: zeros, pipeline_mode=pl.Buffered(1))


def _rms_norm_rows(x, gain):
    ms = jnp.mean(x * x, axis=-1, keepdims=True)
    return x * lax.rsqrt(ms + EPS) * gain


def _dot(a, b):
    return jnp.dot(a, b, preferred_element_type=F32)


def _dot_nt(a, b):
    return lax.dot_general(a, b, (((1,), (1,)), ((), ())), preferred_element_type=F32)


def _neg_abs(x):
    bits = lax.bitcast_convert_type(x, jnp.uint32) | jnp.uint32(0x80000000)
    return lax.bitcast_convert_type(bits, F32)


def _split_bf16(x):
    hi = x.astype(BF16)
    lo = (x - hi.astype(F32)).astype(BF16)
    return hi, lo


def _ffn_kernel(*refs, d_ff, has_attn):
    if has_attn:
        a_ref, wo_ref, x_ref, g_ref, wgu_ref, wd_ref, o_ref = refs
        x = x_ref[...] + _dot(a_ref[...], wo_ref[...])
    else:
        x_ref, g_ref, wgu_ref, wd_ref, o_ref = refs
        x = x_ref[...]
    h = _rms_norm_rows(x, g_ref[...]).astype(BF16)
    y = jnp.zeros_like(x)
    for c in range(d_ff // FF_TILE):
        lo = c * FF_TILE
        gate = _dot(h, wgu_ref[:, lo:lo + FF_TILE])
        up = _dot(h, wgu_ref[:, d_ff + lo:d_ff + lo + FF_TILE])
        act = gate * (1.0 / (1.0 + jnp.exp(-gate))) * up
        y = y + _dot(act.astype(BF16), wd_ref[lo:lo + FF_TILE, :])
    o_ref[...] = x + 0.5 * y


def _layer_weights(shape, *index):
    lead = len(index)
    return pl.BlockSpec((None,) * lead + tuple(shape), lambda *_: tuple(index) + (0, 0),
                        pipeline_mode=pl.Buffered(1))


def _ffn(x, gain, w_gate_up, w_down, layer, which, attn=None, w_out=None, slot=None):
    m, d = x.shape
    d_ff = w_down.shape[-2]
    row = lambda i: (i, 0)
    in_specs, args = [], []
    if attn is not None:
        k = attn.shape[1]
        in_specs += [pl.BlockSpec((ROW_TILE, k), row), _layer_weights((k, d), slot)]
        args += [attn, w_out]
    in_specs += [
        pl.BlockSpec((ROW_TILE, d), row),
        _resident((1, d)),
        _layer_weights((d, 2 * d_ff), layer, which),
        _layer_weights((d_ff, d), layer, which),
    ]
    args += [x, gain, w_gate_up, w_down]
    return pl.pallas_call(
        functools.partial(_ffn_kernel, d_ff=d_ff, has_attn=attn is not None),
        grid=(m // ROW_TILE,),
        in_specs=in_specs,
        out_specs=pl.BlockSpec((ROW_TILE, d), row),
        out_shape=jax.ShapeDtypeStruct((m, d), F32),
        compiler_params=_params(1),
        name="ffn",
    )(*args)


def _norm_proj_kernel(x_ref, g_ref, w_ref, o_ref, *, n_scaled, scale):
    h = _rms_norm_rows(x_ref[...], g_ref[...]).astype(BF16)
    y = _dot(h, w_ref[...])
    o_ref[:, :n_scaled] = (y[:, :n_scaled] * scale).astype(o_ref.dtype)
    o_ref[:, n_scaled:] = y[:, n_scaled:].astype(o_ref.dtype)


def _norm_proj(x, gain, w, *, n_scaled, scale):
    m, d = x.shape
    n = w.shape[1]
    return pl.pallas_call(
        functools.partial(_norm_proj_kernel, n_scaled=n_scaled, scale=scale),
        grid=(m // ROW_TILE,),
        in_specs=[
            pl.BlockSpec((ROW_TILE, d), lambda i: (i, 0)),
            _resident((1, d)),
            _resident((d, n)),
        ],
        out_specs=pl.BlockSpec((ROW_TILE, n), lambda i: (i, 0)),
        out_shape=jax.ShapeDtypeStruct((m, n), BF16),
        compiler_params=_params(1),
        name="norm_proj",
    )(x, gain, w)


def _sb_attn_kernel(tq_ref, tj_ref, ts_ref, q_ref, k_ref, v_ref, tri_ref, o_ref,
                    acc_ref, c_ref, qs_ref, z0_ref, z1_ref, a0_ref, a1_ref, *, n_blk, n_groups):
    blk = SB_BLOCK
    lane = lax.broadcasted_iota(jnp.int32, (blk, LANES), 1)
    first = lane < HEAD_DIM

    def logits(qis, kjs):
        return [_dot_nt(qs_ref[i], k_ref[pl.ds(pl.multiple_of(j * blk, blk), blk), :])
                for i, j in zip(qis, kjs)]

    def log_weights(zs, carries, causal):
        sps = [jnp.maximum(z, 0.0) + jnp.log2(1.0 + jnp.exp2(_neg_abs(z))) for z in zs]
        if causal is not None:
            sps = [jnp.where(causal, sp, 0.0) for sp in sps]
        laters = [_dot(sp.astype(BF16), tri_ref[...]) for sp in sps]
        args = [z + later for z, later in zip(zs, laters)]
        if carries is not None:
            args = [arg + jnp.concatenate([carry] * (blk // LANES), axis=1)
                    for arg, carry in zip(args, carries)]
        if causal is not None:
            args = [jnp.where(causal, arg, -jnp.inf) for arg in args]
        return args, [jnp.broadcast_to(later[:, 0:1], (2 * blk, LANES)) for later in laters]

    def values(args, kjs):
        vbs = [v_ref[pl.ds(pl.multiple_of(j * blk, blk), blk), :] for j in kjs]
        return [_dot(jnp.exp2(arg).astype(BF16), vb) for arg, vb in zip(args, vbs)]

    q = q_ref[...].reshape(n_blk, blk, LANES)
    zero = jnp.zeros_like(q)
    qs_ref[:, 0:blk, :] = jnp.where(first[None], q, zero)
    qs_ref[:, blk:2 * blk, :] = jnp.where(first[None], zero, q)

    row = lax.broadcasted_iota(jnp.int32, (2 * blk, blk), 0)
    row = jnp.where(row >= blk, row - blk, row)
    col = lax.broadcasted_iota(jnp.int32, (2 * blk, blk), 1)
    causal = col < row

    def diag_body(it, _):
        idx = [it * SB_DIAG_GROUP + u for u in range(SB_DIAG_GROUP)]
        args, totals = log_weights(logits(idx, idx), None, causal)
        pvs = values(args, idx)
        for i, pv, total in zip(idx, pvs, totals):
            acc_ref[i] = pv
            c_ref[i] = total
        return 0

    lax.fori_loop(0, n_blk // SB_DIAG_GROUP, diag_body, 0)
    acc_ref[n_blk] = jnp.zeros(acc_ref.shape[1:], F32)
    c_ref[n_blk] = jnp.zeros(c_ref.shape[1:], F32)

    def group(g):
        return [g * SB_GROUP + u for u in range(SB_GROUP)]

    def logit_stage(g, zw_ref):
        ts = group(g)
        for u, z in enumerate(logits([tq_ref[t] for t in ts], [tj_ref[t] for t in ts])):
            zw_ref[u] = z

    def weight_stage(g, zr_ref, aw_ref):
        slots = [ts_ref[t] for t in group(g)]
        carries = [c_ref[slot] for slot in slots]
        args, totals = log_weights([zr_ref[u] for u in range(SB_GROUP)], carries, None)
        for u, (slot, carry, arg, total) in enumerate(zip(slots, carries, args, totals)):
            aw_ref[u] = arg
            c_ref[slot] = carry + total

    def value_stage(g, ar_ref):
        ts = group(g)
        pvs = values([ar_ref[u] for u in range(SB_GROUP)], [tj_ref[t] for t in ts])
        for t, pv in zip(ts, pvs):
            acc_ref[ts_ref[t]] += pv

    z_refs = (z0_ref, z1_ref)
    a_refs = (a0_ref, a1_ref)

    def step(g, parity):
        logit_stage(g, z_refs[parity])
        weight_stage(g - 1, z_refs[1 - parity], a_refs[1 - parity])
        value_stage(g - 2, a_refs[parity])

    logit_stage(0, z_refs[0])
    logit_stage(1, z_refs[1])
    weight_stage(0, z_refs[0], a_refs[0])
    n_bodies = (n_groups - 2) // SB_STEPS

    def off_body(h, _):
        for k in range(SB_STEPS):
            step(2 + SB_STEPS * h + k, k % 2)
        return 0

    lax.fori_loop(0, n_bodies, off_body, 0)
    for g in range(2 + SB_STEPS * n_bodies, n_groups):
        step(g, g % 2)
    last = (n_groups - 1) % 2
    weight_stage(n_groups - 1, z_refs[last], a_refs[last])
    value_stage(n_groups - 2, a_refs[1 - last])
    value_stage(n_groups - 1, a_refs[last])

    acc = acc_ref[0:n_blk]
    o = jnp.where(first[None], acc[:, :blk], acc[:, blk:])
    o_ref[...] = o.reshape(n_blk * blk, LANES).astype(o_ref.dtype)


def _sb_tile_order(n_blk):
    dummy = (0, 0, n_blk)
    order = []
    for d in range(1, n_blk):
        diag = [(i, i - d, i) for i in range(d, n_blk)]
        room = -len(order) % SB_GROUP
        if any(i in {t[2] for t in order[len(order) - (SB_GROUP - room):]} for i, _, _ in diag[:room]):
            order += [dummy] * room
        order += diag
    order += [dummy] * (-len(order) % SB_GROUP)
    while len(order) // SB_GROUP < 2 or (len(order) // SB_GROUP - 2) % SB_STEPS:
        order += [dummy] * SB_GROUP
    return order


def _sb_attention(qkv, tri, *, n_heads):
    b, s, _ = qkv.shape
    n_pairs = n_heads * HEAD_DIM // LANES
    blk = SB_BLOCK
    n_blk = s // blk
    assert n_blk % SB_DIAG_GROUP == 0 and SB_STEPS % 2 == 0
    order = _sb_tile_order(n_blk)
    tables = [jnp.array([t[c] for t in order], jnp.int32) for c in range(3)]
    seq = lambda col: pl.BlockSpec((None, s, LANES), lambda bi, h, *_: (bi, 0, col * n_pairs + h))
    grid_spec = pltpu.PrefetchScalarGridSpec(
        num_scalar_prefetch=3,
        grid=(b, n_pairs),
        in_specs=[
            seq(0), seq(1), seq(2),
            pl.BlockSpec((blk, blk), lambda bi, h, *_: (0, 0), pipeline_mode=pl.Buffered(1)),
        ],
        out_specs=pl.BlockSpec((None, s, LANES), lambda bi, h, *_: (bi, 0, h)),
        scratch_shapes=[
            pltpu.VMEM((n_blk + 1, 2 * blk, LANES), F32),
            pltpu.VMEM((n_blk + 1, 2 * blk, LANES), F32),
            pltpu.VMEM((n_blk, 2 * blk, LANES), BF16),
        ] + [pltpu.VMEM((SB_GROUP, 2 * blk, blk), F32)] * 4,
    )
    return pl.pallas_call(
        functools.partial(_sb_attn_kernel, n_blk=n_blk, n_groups=len(order) // SB_GROUP),
        grid_spec=grid_spec,
        out_shape=jax.ShapeDtypeStruct((b, s, n_pairs * LANES), BF16),
        compiler_params=_params(2),
        name="sb_attention",
    )(*tables, qkv, qkv, qkv, tri)


def _rope_table_kernel(pos_ref, invf_ref, cos_ref, sin_ref):
    ang = pos_ref[...].astype(F32) * invf_ref[...]
    cos_ref[...] = jnp.cos(ang)
    sin_ref[...] = jnp.sin(ang)


def _rope_tables(positions):
    b, s = positions.shape
    half = ROT_DIM // 2
    rows = s * half // LANES
    inv_freq = ROPE_THETA ** (-jnp.arange(0, ROT_DIM, 2, dtype=F32) / ROT_DIM)
    invf = jnp.tile(inv_freq, LANES // half).reshape(1, LANES)
    pos = jnp.repeat(positions, half, axis=1).reshape(b, rows, LANES)
    spec = pl.BlockSpec((None, rows, LANES), lambda i: (i, 0, 0))
    cos, sin = pl.pallas_call(
        _rope_table_kernel,
        grid=(b,),
        in_specs=[spec, _resident((1, LANES))],
        out_specs=[spec, spec],
        out_shape=[jax.ShapeDtypeStruct((b, rows, LANES), F32)] * 2,
        compiler_params=_params(1),
        name="rope_tables",
    )(pos, invf)
    cos = cos.reshape(b, s, half)
    sin = sin.reshape(b, s, half)
    rest = HEAD_DIM - ROT_DIM
    ones = jnp.ones((b, s, rest), F32)
    zeros = jnp.zeros((b, s, rest + half), F32)
    reps = LANES // HEAD_DIM
    cos_t = jnp.tile(jnp.concatenate([cos, cos, ones], axis=-1), (1, 1, reps))
    sin_up = jnp.tile(jnp.concatenate([sin, zeros], axis=-1), (1, 1, reps))
    sin_dn = jnp.tile(jnp.concatenate([zeros[..., :half], sin, zeros[..., :rest]], axis=-1),
                      (1, 1, reps))
    m = b * s
    return cos_t.reshape(m, LANES), sin_up.reshape(m, LANES), sin_dn.reshape(m, LANES)


def _sw_proj_kernel(x_ref, g_ref, w_ref, cos_ref, sup_ref, sdn_ref, hg_ref, seg_ref, o_ref,
                    *, n_q_groups, n_k_groups, n_groups):
    h = _rms_norm_rows(x_ref[...], g_ref[...]).astype(BF16)
    y = _dot(h, w_ref[...])
    cos_t = cos_ref[...]
    sin_up = sup_ref[...]
    sin_dn = sdn_ref[...]
    seg2 = seg_ref[...]
    for grp in range(n_groups):
        yg = y[:, grp * LANES:(grp + 1) * LANES]
        if grp < n_q_groups + n_k_groups:
            is_q = grp < n_q_groups
            hi, lo = _split_bf16(yg * yg)
            ss = _dot(jnp.concatenate([hi, lo], axis=1), seg2)
            gain = hg_ref[0:1, :] if is_q else hg_ref[1:2, :]
            yn = yg * lax.rsqrt(ss * (1.0 / HEAD_DIM) + EPS) * gain
            up = pltpu.roll(yn, LANES - ROT_DIM // 2, 1)
            dn = pltpu.roll(yn, ROT_DIM // 2, 1)
            r = yn * cos_t - up * sin_up + dn * sin_dn
            if is_q:
                r = r * (HEAD_DIM ** -0.5)
            o_ref[:, grp * LANES:(grp + 1) * LANES] = r.astype(o_ref.dtype)
        else:
            o_ref[:, grp * LANES:(grp + 1) * LANES] = yg.astype(o_ref.dtype)


def _sw_proj(x, gain, w, cos_t, sin_up, sin_dn, head_gains, seg2, *, n_q_groups, n_k_groups):
    m, d = x.shape
    n = w.shape[1]
    row = lambda i: (i, 0)
    return pl.pallas_call(
        functools.partial(_sw_proj_kernel, n_q_groups=n_q_groups, n_k_groups=n_k_groups,
                          n_groups=n // LANES),
        grid=(m // ROW_TILE,),
        in_specs=[
            pl.BlockSpec((ROW_TILE, d), row),
            _resident((1, d)),
            _resident((d, n)),
            pl.BlockSpec((ROW_TILE, LANES), row),
            pl.BlockSpec((ROW_TILE, LANES), row),
            pl.BlockSpec((ROW_TILE, LANES), row),
            _resident((2, LANES)),
            _resident((2 * LANES, LANES)),
        ],
        out_specs=pl.BlockSpec((ROW_TILE, n), row),
        out_shape=jax.ShapeDtypeStruct((m, n), BF16),
        compiler_params=_params(1),
        name="sw_proj",
    )(x, gain, w, cos_t, sin_up, sin_dn, head_gains, seg2)


def _swa_attn_kernel(q_ref, kc_ref, kp_ref, vc_ref, vp_ref, sink_ref, o_ref):
    blk = SW_BLOCK
    n = pl.program_id(1)
    lane = lax.broadcasted_iota(jnp.int32, (blk, LANES), 1)
    first = lane < HEAD_DIM
    t = lax.broadcasted_iota(jnp.int32, (blk, 2 * blk), 0)
    j = lax.broadcasted_iota(jnp.int32, (blk, 2 * blk), 1)
    band = (j > t) & (j <= t + blk)
    ones = jnp.ones((2 * blk, LANES), BF16)
    chains = [(qb, kh) for qb in range(SW_STEP_BLOCKS) for kh in range(SW_KV_HEADS)]
    masks, kbs, vbs, lhs = [], [], [], []
    for qb, kh in chains:
        cols = slice(kh * LANES, (kh + 1) * LANES)
        rows_cur = slice(qb * blk, (qb + 1) * blk)
        if qb == 0:
            k_prev, v_prev = kp_ref[:, cols], vp_ref[:, cols]
            masks.append(band & ((j >= blk) | (n > 0)))
        else:
            rows_prev = slice((qb - 1) * blk, qb * blk)
            k_prev, v_prev = kc_ref[rows_prev, cols], vc_ref[rows_prev, cols]
            masks.append(band)
        kbs.append(jnp.concatenate([k_prev, kc_ref[rows_cur, cols]], axis=0))
        vbs.append(jnp.concatenate([v_prev, vc_ref[rows_cur, cols]], axis=0))
        rows = []
        for g in range(SW_GROUP):
            pair = (kh * SW_GROUP + g) // 2
            qp = q_ref[rows_cur, pair * LANES:(pair + 1) * LANES]
            zero = jnp.zeros_like(qp)
            rows.append(jnp.where(first, qp, zero) if g % 2 == 0 else jnp.where(first, zero, qp))
        lhs.append(jnp.concatenate(rows, axis=0))
    ss = [_dot_nt(l, kb).reshape(SW_GROUP, blk, 2 * blk) for l, kb in zip(lhs, kbs)]
    ss = [jnp.where(mask[None], s, -jnp.inf) for s, mask in zip(ss, masks)]
    sinks = [sink_ref[kh] for _, kh in chains]
    ms = [jnp.maximum(jnp.max(s, axis=-1, keepdims=True), sink) for s, sink in zip(ss, sinks)]
    ms = [jnp.broadcast_to(m, (SW_GROUP, blk, LANES)) for m in ms]
    ps = [jnp.exp(s - jnp.concatenate([m, m], axis=-1)).astype(BF16).reshape(SW_GROUP * blk, 2 * blk)
          for s, m in zip(ss, ms)]
    denoms = [_dot(p, ones) + jnp.exp(sink - m).reshape(SW_GROUP * blk, LANES)
              for p, sink, m in zip(ps, sinks, ms)]
    outs = [_dot(p, vb) / denom for p, vb, denom in zip(ps, vbs, denoms)]
    for (qb, kh), o in zip(chains, outs):
        for g2 in range(SW_GROUP // 2):
            pair = kh * (SW_GROUP // 2) + g2
            even = o[(2 * g2) * blk:(2 * g2 + 1) * blk]
            odd = o[(2 * g2 + 1) * blk:(2 * g2 + 2) * blk]
            o_ref[qb * blk:(qb + 1) * blk, pair * LANES:(pair + 1) * LANES] = (
                jnp.where(first, even, odd).astype(o_ref.dtype))


def _swa_attention(qkv, sinks, *, n_q_groups):
    b, s, _ = qkv.shape
    blk = SW_BLOCK
    step = SW_STEP_BLOCKS * blk
    d_q = n_q_groups * LANES
    kv_w = SW_KV_HEADS * LANES
    k_col = d_q // kv_w
    v_col = k_col + 1
    cur = lambda col: (lambda bi, n: (bi, n, col))
    prev = lambda col: (lambda bi, n: (bi, jnp.maximum(n * SW_STEP_BLOCKS - 1, 0), col))
    return pl.pallas_call(
        _swa_attn_kernel,
        grid=(b, s // step),
        in_specs=[
            pl.BlockSpec((None, step, d_q), cur(0)),
            pl.BlockSpec((None, step, kv_w), cur(k_col)),
            pl.BlockSpec((None, blk, kv_w), prev(k_col)),
            pl.BlockSpec((None, step, kv_w), cur(v_col)),
            pl.BlockSpec((None, blk, kv_w), prev(v_col)),
            _resident((SW_KV_HEADS, SW_GROUP, 1, 1)),
        ],
        out_specs=pl.BlockSpec((None, step, d_q), cur(0)),
        out_shape=jax.ShapeDtypeStruct((b, s, d_q), BF16),
        compiler_params=_params(2),
        name="swa_attention",
    )(qkv, qkv, qkv, qkv, qkv, sinks)


def _dup_heads(w, n_heads):
    d = w.shape[0]
    w = w.reshape(d, n_heads, 1, HEAD_DIM)
    return jnp.broadcast_to(w, (d, n_heads, LANES // HEAD_DIM, HEAD_DIM)).reshape(d, n_heads * LANES)


def kernel(x, positions, norm_gains, ffn_w_gate_up, ffn_w_down, sb_w_in, sb_w_out,
           sw_w_in, sw_w_out, sw_q_gain, sw_k_gain, sw_sinks):
    b, s, d = x.shape
    m = b * s
    depth = norm_gains.shape[0]
    sb_heads = sb_w_out.shape[1] // HEAD_DIM
    d_q = sw_w_out.shape[1]
    n_q_groups = d_q // LANES
    d_kv = SW_KV_HEADS * HEAD_DIM

    cos_t, sin_up, sin_dn = _rope_tables(positions)
    idx = jnp.arange(SB_BLOCK)
    tri = jnp.where(idx[:, None] >= idx[None, :], -1.0, 0.0).astype(BF16)
    lane_head = jnp.arange(LANES) // HEAD_DIM
    seg = (lane_head[:, None] == lane_head[None, :]).astype(BF16)
    seg2 = jnp.concatenate([seg, seg], axis=0)

    w_gate_up = ffn_w_gate_up.astype(BF16)
    w_down = ffn_w_down.astype(BF16)
    w_out = (sb_w_out.astype(BF16), sw_w_out.astype(BF16))

    h = x.reshape(m, d)
    for i in range(depth):
        slot = i // 2
        gains = norm_gains[i].reshape(3, 1, d)
        h = _ffn(h, gains[0], w_gate_up, w_down, i, 0)
        if i % 2 == 0:
            qkv = _norm_proj(h, gains[1], sb_w_in[slot].astype(BF16),
                             n_scaled=sb_heads * HEAD_DIM, scale=HEAD_DIM ** -0.5 * LOG2_E)
            o = _sb_attention(qkv.reshape(b, s, -1), tri, n_heads=sb_heads)
        else:
            w_in = sw_w_in[slot]
            w_ext = jnp.concatenate(
                [w_in[:, :d_q],
                 _dup_heads(w_in[:, d_q:d_q + d_kv], SW_KV_HEADS),
                 _dup_heads(w_in[:, d_q + d_kv:], SW_KV_HEADS)], axis=1).astype(BF16)
            reps = LANES // HEAD_DIM
            head_gains = jnp.stack([jnp.tile(sw_q_gain[slot], reps), jnp.tile(sw_k_gain[slot], reps)])
            qkv = _sw_proj(h, gains[1], w_ext, cos_t, sin_up, sin_dn, head_gains, seg2,
                           n_q_groups=n_q_groups, n_k_groups=SW_KV_HEADS)
            sinks = sw_sinks[slot].reshape(SW_KV_HEADS, SW_GROUP, 1, 1)
            o = _swa_attention(qkv.reshape(b, s, -1), sinks, n_q_groups=n_q_groups)
        h = _ffn(h, gains[2], w_gate_up, w_down, i, 1,
                 attn=o.reshape(m, -1), w_out=w_out[i % 2], slot=slot)
    return h.reshape(b, s, d)
```

```python
import functools

import jax
import jax.numpy as jnp
from jax import lax
from jax.experimental import pallas as pl
from jax.experimental.pallas import tpu as pltpu

F32 = jnp.float32
BF16 = jnp.bfloat16

HEAD_DIM = 64
LANES = 128
SW_KV_HEADS = 2
SW_GROUP = 8
SW_BLOCK = 128
SW_STEP_BLOCKS = 4
SB_BLOCK = 256
SB_DIAG_GROUP = 4
SB_GROUP = 1
SB_STEPS = 24
ROPE_THETA = 500000.0
ROT_DIM = HEAD_DIM // 4
EPS = 1e-6
LOG2_E = 1.4426950408889634
VMEM_LIMIT_BYTES = 56 * 1024 * 1024
ROW_TILE = 512
FF_TILE = 256


def _params(n_axes):
    return pltpu.CompilerParams(
        dimension_semantics=("arbitrary",) * n_axes,
        vmem_limit_bytes=VMEM_LIMIT_BYTES)


def _resident(shape):
    zeros = (0,) * len(shape)
    return pl.BlockSpec(shape, lambda *_: zeros, pipeline_mode=pl.Buffered(1))


def _rms_norm_rows(x, gain):
    ms = jnp.mean(x * x, axis=-1, keepdims=True)
    return x * lax.rsqrt(ms + EPS) * gain


def _dot(a, b):
    return jnp.dot(a, b, preferred_element_type=F32)


def _dot_nt(a, b):
    return lax.dot_general(a, b, (((1,), (1,)), ((), ())), preferred_element_type=F32)


def _neg_abs(x):
    bits = lax.bitcast_convert_type(x, jnp.uint32) | jnp.uint32(0x80000000)
    return lax.bitcast_convert_type(bits, F32)


def _split_bf16(x):
    hi = x.astype(BF16)
    lo = (x - hi.astype(F32)).astype(BF16)
    return hi, lo


def _ffn_kernel(*refs, d_ff, has_attn):
    if has_attn:
        a_ref, wo_ref, x_ref, g_ref, wgu_ref, wd_ref, o_ref = refs
        x = x_ref[...] + _dot(a_ref[...], wo_ref[...])
    else:
        x_ref, g_ref, wgu_ref, wd_ref, o_ref = refs
        x = x_ref[...]
    h = _rms_norm_rows(x, g_ref[...]).astype(BF16)
    y = jnp.zeros_like(x)
    for c in range(d_ff // FF_TILE):
        lo = c * FF_TILE
        gate = _dot(h, wgu_ref[:, lo:lo + FF_TILE])
        up = _dot(h, wgu_ref[:, d_ff + lo:d_ff + lo + FF_TILE])
        act = gate * (1.0 / (1.0 + jnp.exp(-gate))) * up
        y = y + _dot(act.astype(BF16), wd_ref[lo:lo + FF_TILE, :])
    o_ref[...] = x + 0.5 * y


def _layer_weights(shape, *index):
    lead = len(index)
    return pl.BlockSpec((None,) * lead + tuple(shape), lambda *_: tuple(index) + (0, 0),
                        pipeline_mode=pl.Buffered(1))


def _ffn(x, gain, w_gate_up, w_down, layer, which, attn=None, w_out=None, slot=None):
    m, d = x.shape
    d_ff = w_down.shape[-2]
    row = lambda i: (i, 0)
    in_specs, args = [], []
    if attn is not None:
        k = attn.shape[1]
        in_specs += [pl.BlockSpec((ROW_TILE, k), row), _layer_weights((k, d), slot)]
        args += [attn, w_out]
    in_specs += [
        pl.BlockSpec((ROW_TILE, d), row),
        _resident((1, d)),
        _layer_weights((d, 2 * d_ff), layer, which),
        _layer_weights((d_ff, d), layer, which),
    ]
    args += [x, gain, w_gate_up, w_down]
    return pl.pallas_call(
        functools.partial(_ffn_kernel, d_ff=d_ff, has_attn=attn is not None),
        grid=(m // ROW_TILE,),
        in_specs=in_specs,
        out_specs=pl.BlockSpec((ROW_TILE, d), row),
        out_shape=jax.ShapeDtypeStruct((m, d), F32),
        compiler_params=_params(1),
        name="ffn",
    )(*args)


def _norm_proj_kernel(x_ref, g_ref, w_ref, o_ref, *, n_scaled, scale):
    h = _rms_norm_rows(x_ref[...], g_ref[...]).astype(BF16)
    y = _dot(h, w_ref[...])
    o_ref[:, :n_scaled] = (y[:, :n_scaled] * scale).astype(o_ref.dtype)
    o_ref[:, n_scaled:] = y[:, n_scaled:].astype(o_ref.dtype)


def _norm_proj(x, gain, w, *, n_scaled, scale):
    m, d = x.shape
    n = w.shape[1]
    return pl.pallas_call(
        functools.partial(_norm_proj_kernel, n_scaled=n_scaled, scale=scale),
        grid=(m // ROW_TILE,),
        in_specs=[
            pl.BlockSpec((ROW_TILE, d), lambda i: (i, 0)),
            _resident((1, d)),
            _resident((d, n)),
        ],
        out_specs=pl.BlockSpec((ROW_TILE, n), lambda i: (i, 0)),
        out_shape=jax.ShapeDtypeStruct((m, n), BF16),
        compiler_params=_params(1),
        name="norm_proj",
    )(x, gain, w)


def _sb_attn_kernel(tq_ref, tj_ref, ts_ref, q_ref, k_ref, v_ref, tri_ref, o_ref,
                    acc_ref, c_ref, qs_ref, z0_ref, z1_ref, a0_ref, a1_ref, *, n_blk, n_groups):
    blk = SB_BLOCK
    lane = lax.broadcasted_iota(jnp.int32, (blk, LANES), 1)
    first = lane < HEAD_DIM

    def logits(qis, kjs):
        return [_dot_nt(qs_ref[i], k_ref[pl.ds(pl.multiple_of(j * blk, blk), blk), :])
                for i, j in zip(qis, kjs)]

    def log_weights(zs, carries, causal):
        sps = [jnp.maximum(z, 0.0) + jnp.log2(1.0 + jnp.exp2(_neg_abs(z))) for z in zs]
        if causal is not None:
            sps = [jnp.where(causal, sp, 0.0) for sp in sps]
        laters = [_dot(sp.astype(BF16), tri_ref[...]) for sp in sps]
        args = [z + later for z, later in zip(zs, laters)]
        if carries is not None:
            args = [arg + jnp.concatenate([carry] * (blk // LANES), axis=1)
                    for arg, carry in zip(args, carries)]
        if causal is not None:
            args = [jnp.where(causal, arg, -jnp.inf) for arg in args]
        return args, [jnp.broadcast_to(later[:, 0:1], (2 * blk, LANES)) for later in laters]

    def values(args, kjs):
        vbs = [v_ref[pl.ds(pl.multiple_of(j * blk, blk), blk), :] for j in kjs]
        return [_dot(jnp.exp2(arg).astype(BF16), vb) for arg, vb in zip(args, vbs)]

    q = q_ref[...].reshape(n_blk, blk, LANES)
    zero = jnp.zeros_like(q)
    qs_ref[:, 0:blk, :] = jnp.where(first[None], q, zero)
    qs_ref[:, blk:2 * blk, :] = jnp.where(first[None], zero, q)

    row = lax.broadcasted_iota(jnp.int32, (2 * blk, blk), 0)
    row = jnp.where(row >= blk, row - blk, row)
    col = lax.broadcasted_iota(jnp.int32, (2 * blk, blk), 1)
    causal = col < row

    def diag_body(it, _):
        idx = [it * SB_DIAG_GROUP + u for u in range(SB_DIAG_GROUP)]
        args, totals = log_weights(logits(idx, idx), None, causal)
        pvs = values(args, idx)
        for i, pv, total in zip(idx, pvs, totals):
            acc_ref[i] = pv
            c_ref[i] = total
        return 0

    lax.fori_loop(0, n_blk // SB_DIAG_GROUP, diag_body, 0)
    acc_ref[n_blk] = jnp.zeros(acc_ref.shape[1:], F32)
    c_ref[n_blk] = jnp.zeros(c_ref.shape[1:], F32)

    def group(g):
        return [g * SB_GROUP + u for u in range(SB_GROUP)]

    def logit_stage(g, zw_ref):
        ts = group(g)
        for u, z in enumerate(logits([tq_ref[t] for t in ts], [tj_ref[t] for t in ts])):
            zw_ref[u] = z

    def weight_stage(g, zr_ref, aw_ref):
        slots = [ts_ref[t] for t in group(g)]
        carries = [c_ref[slot] for slot in slots]
        args, totals = log_weights([zr_ref[u] for u in range(SB_GROUP)], carries, None)
        for u, (slot, carry, arg, total) in enumerate(zip(slots, carries, args, totals)):
            aw_ref[u] = arg
            c_ref[slot] = carry + total

    def value_stage(g, ar_ref):
        ts = group(g)
        pvs = values([ar_ref[u] for u in range(SB_GROUP)], [tj_ref[t] for t in ts])
        for t, pv in zip(ts, pvs):
            acc_ref[ts_ref[t]] += pv

    z_refs = (z0_ref, z1_ref)
    a_refs = (a0_ref, a1_ref)

    def step(g, parity):
        logit_stage(g, z_refs[parity])
        value_stage(g - 2, a_refs[parity])
        weight_stage(g - 1, z_refs[1 - parity], a_refs[1 - parity])

    logit_stage(0, z_refs[0])
    logit_stage(1, z_refs[1])
    weight_stage(0, z_refs[0], a_refs[0])
    n_bodies = (n_groups - 2) // SB_STEPS

    def off_body(h, _):
        for k in range(SB_STEPS):
            step(2 + SB_STEPS * h + k, k % 2)
        return 0

    lax.fori_loop(0, n_bodies, off_body, 0)
    for g in range(2 + SB_STEPS * n_bodies, n_groups):
        step(g, g % 2)
    last = (n_groups - 1) % 2
    weight_stage(n_groups - 1, z_refs[last], a_refs[last])
    value_stage(n_groups - 2, a_refs[1 - last])
    value_stage(n_groups - 1, a_refs[last])

    acc = acc_ref[0:n_blk]
    o = jnp.where(first[None], acc[:, :blk], acc[:, blk:])
    o_ref[...] = o.reshape(n_blk * blk, LANES).astype(o_ref.dtype)


def _sb_tile_order(n_blk):
    dummy = (0, 0, n_blk)
    order = []
    for d in range(1, n_blk):
        diag = [(i, i - d, i) for i in range(d, n_blk)]
        room = -len(order) % SB_GROUP
        if any(i in {t[2] for t in order[len(order) - (SB_GROUP - room):]} for i, _, _ in diag[:room]):
            order += [dummy] * room
        order += diag
    order += [dummy] * (-len(order) % SB_GROUP)
    while len(order) // SB_GROUP < 2 or (len(order) // SB_GROUP - 2) % SB_STEPS:
        order += [dummy] * SB_GROUP
    return order


def _sb_attention(qkv, tri, *, n_heads):
    b, s, _ = qkv.shape
    n_pairs = n_heads * HEAD_DIM // LANES
    blk = SB_BLOCK
    n_blk = s // blk
    assert n_blk % SB_DIAG_GROUP == 0 and SB_STEPS % 2 == 0
    order = _sb_tile_order(n_blk)
    tables = [jnp.array([t[c] for t in order], jnp.int32) for c in range(3)]
    seq = lambda col: pl.BlockSpec((None, s, LANES), lambda bi, h, *_: (bi, 0, col * n_pairs + h))
    grid_spec = pltpu.PrefetchScalarGridSpec(
        num_scalar_prefetch=3,
        grid=(b, n_pairs),
        in_specs=[
            seq(0), seq(1), seq(2),
            pl.BlockSpec((blk, blk), lambda bi, h, *_: (0, 0), pipeline_mode=pl.Buffered(1)),
        ],
        out_specs=pl.BlockSpec((None, s, LANES), lambda bi, h, *_: (bi, 0, h)),
        scratch_shapes=[
            pltpu.VMEM((n_blk + 1, 2 * blk, LANES), F32),
            pltpu.VMEM((n_blk + 1, 2 * blk, LANES), F32),
            pltpu.VMEM((n_blk, 2 * blk, LANES), BF16),
        ] + [pltpu.VMEM((SB_GROUP, 2 * blk, blk), F32)] * 4,
    )
    return pl.pallas_call(
        functools.partial(_sb_attn_kernel, n_blk=n_blk, n_groups=len(order) // SB_GROUP),
        grid_spec=grid_spec,
        out_shape=jax.ShapeDtypeStruct((b, s, n_pairs * LANES), BF16),
        compiler_params=_params(2),
        name="sb_attention",
    )(*tables, qkv, qkv, qkv, tri)


def _rope_table_kernel(pos_ref, invf_ref, cos_ref, sin_ref):
    ang = pos_ref[...].astype(F32) * invf_ref[...]
    cos_ref[...] = jnp.cos(ang)
    sin_ref[...] = jnp.sin(ang)


def _rope_tables(positions):
    b, s = positions.shape
    half = ROT_DIM // 2
    rows = s * half // LANES
    inv_freq = ROPE_THETA ** (-jnp.arange(0, ROT_DIM, 2, dtype=F32) / ROT_DIM)
    invf = jnp.tile(inv_freq, LANES // half).reshape(1, LANES)
    pos = jnp.repeat(positions, half, axis=1).reshape(b, rows, LANES)
    spec = pl.BlockSpec((None, rows, LANES), lambda i: (i, 0, 0))
    cos, sin = pl.pallas_call(
        _rope_table_kernel,
        grid=(b,),
        in_specs=[spec, _resident((1, LANES))],
        out_specs=[spec, spec],
        out_shape=[jax.ShapeDtypeStruct((b, rows, LANES), F32)] * 2,
        compiler_params=_params(1),
        name="rope_tables",
    )(pos, invf)
    cos = cos.reshape(b, s, half)
    sin = sin.reshape(b, s, half)
    rest = HEAD_DIM - ROT_DIM
    ones = jnp.ones((b, s, rest), F32)
    zeros = jnp.zeros((b, s, rest + half), F32)
    reps = LANES // HEAD_DIM
    cos_t = jnp.tile(jnp.concatenate([cos, cos, ones], axis=-1), (1, 1, reps))
    sin_up = jnp.tile(jnp.concatenate([sin, zeros], axis=-1), (1, 1, reps))
    sin_dn = jnp.tile(jnp.concatenate([zeros[..., :half], sin, zeros[..., :rest]], axis=-1),
                      (1, 1, reps))
    m = b * s
    return cos_t.reshape(m, LANES), sin_up.reshape(m, LANES), sin_dn.reshape(m, LANES)


def _sw_proj_kernel(x_ref, g_ref, w_ref, cos_ref, sup_ref, sdn_ref, hg_ref, seg_ref, o_ref,
                    *, n_q_groups, n_k_groups, n_groups):
    h = _rms_norm_rows(x_ref[...], g_ref[...]).astype(BF16)
    y = _dot(h, w_ref[...])
    cos_t = cos_ref[...]
    sin_up = sup_ref[...]
    sin_dn = sdn_ref[...]
    seg2 = seg_ref[...]
    for grp in range(n_groups):
        yg = y[:, grp * LANES:(grp + 1) * LANES]
        if grp < n_q_groups + n_k_groups:
            is_q = grp < n_q_groups
            hi, lo = _split_bf16(yg * yg)
            ss = _dot(jnp.concatenate([hi, lo], axis=1), seg2)
            gain = hg_ref[0:1, :] if is_q else hg_ref[1:2, :]
            yn = yg * lax.rsqrt(ss * (1.0 / HEAD_DIM) + EPS) * gain
            up = pltpu.roll(yn, LANES - ROT_DIM // 2, 1)
            dn = pltpu.roll(yn, ROT_DIM // 2, 1)
            r = yn * cos_t - up * sin_up + dn * sin_dn
            if is_q:
                r = r * (HEAD_DIM ** -0.5)
            o_ref[:, grp * LANES:(grp + 1) * LANES] = r.astype(o_ref.dtype)
        else:
            o_ref[:, grp * LANES:(grp + 1) * LANES] = yg.astype(o_ref.dtype)


def _sw_proj(x, gain, w, cos_t, sin_up, sin_dn, head_gains, seg2, *, n_q_groups, n_k_groups):
    m, d = x.shape
    n = w.shape[1]
    row = lambda i: (i, 0)
    return pl.pallas_call(
        functools.partial(_sw_proj_kernel, n_q_groups=n_q_groups, n_k_groups=n_k_groups,
                          n_groups=n // LANES),
        grid=(m // ROW_TILE,),
        in_specs=[
            pl.BlockSpec((ROW_TILE, d), row),
            _resident((1, d)),
            _resident((d, n)),
            pl.BlockSpec((ROW_TILE, LANES), row),
            pl.BlockSpec((ROW_TILE, LANES), row),
            pl.BlockSpec((ROW_TILE, LANES), row),
            _resident((2, LANES)),
            _resident((2 * LANES, LANES)),
        ],
        out_specs=pl.BlockSpec((ROW_TILE, n), row),
        out_shape=jax.ShapeDtypeStruct((m, n), BF16),
        compiler_params=_params(1),
        name="sw_proj",
    )(x, gain, w, cos_t, sin_up, sin_dn, head_gains, seg2)


def _swa_attn_kernel(q_ref, kc_ref, kp_ref, vc_ref, vp_ref, sink_ref, o_ref):
    blk = SW_BLOCK
    n = pl.program_id(1)
    lane = lax.broadcasted_iota(jnp.int32, (blk, LANES), 1)
    first = lane < HEAD_DIM
    t = lax.broadcasted_iota(jnp.int32, (blk, 2 * blk), 0)
    j = lax.broadcasted_iota(jnp.int32, (blk, 2 * blk), 1)
    band = (j > t) & (j <= t + blk)
    ones = jnp.ones((2 * blk, LANES), BF16)
    chains = [(qb, kh) for qb in range(SW_STEP_BLOCKS) for kh in range(SW_KV_HEADS)]
    masks, kbs, vbs, lhs = [], [], [], []
    for qb, kh in chains:
        cols = slice(kh * LANES, (kh + 1) * LANES)
        rows_cur = slice(qb * blk, (qb + 1) * blk)
        if qb == 0:
            k_prev, v_prev = kp_ref[:, cols], vp_ref[:, cols]
            masks.append(band & ((j >= blk) | (n > 0)))
        else:
            rows_prev = slice((qb - 1) * blk, qb * blk)
            k_prev, v_prev = kc_ref[rows_prev, cols], vc_ref[rows_prev, cols]
            masks.append(band)
        kbs.append(jnp.concatenate([k_prev, kc_ref[rows_cur, cols]], axis=0))
        vbs.append(jnp.concatenate([v_prev, vc_ref[rows_cur, cols]], axis=0))
        rows = []
        for g in range(SW_GROUP):
            pair = (kh * SW_GROUP + g) // 2
            qp = q_ref[rows_cur, pair * LANES:(pair + 1) * LANES]
            zero = jnp.zeros_like(qp)
            rows.append(jnp.where(first, qp, zero) if g % 2 == 0 else jnp.where(first, zero, qp))
        lhs.append(jnp.concatenate(rows, axis=0))
    ss = [_dot_nt(l, kb).reshape(SW_GROUP, blk, 2 * blk) for l, kb in zip(lhs, kbs)]
    ss = [jnp.where(mask[None], s, -jnp.inf) for s, mask in zip(ss, masks)]
    sinks = [sink_ref[kh] for _, kh in chains]
    ms = [jnp.maximum(jnp.max(s, axis=-1, keepdims=True), sink) for s, sink in zip(ss, sinks)]
    ms = [jnp.broadcast_to(m, (SW_GROUP, blk, LANES)) for m in ms]
    ps = [jnp.exp(s - jnp.concatenate([m, m], axis=-1)).astype(BF16).reshape(SW_GROUP * blk, 2 * blk)
          for s, m in zip(ss, ms)]
    denoms = [_dot(p, ones) + jnp.exp(sink - m).reshape(SW_GROUP * blk, LANES)
              for p, sink, m in zip(ps, sinks, ms)]
    outs = [_dot(p, vb) / denom for p, vb, denom in zip(ps, vbs, denoms)]
    for (qb, kh), o in zip(chains, outs):
        for g2 in range(SW_GROUP // 2):
            pair = kh * (SW_GROUP // 2) + g2
            even = o[(2 * g2) * blk:(2 * g2 + 1) * blk]
            odd = o[(2 * g2 + 1) * blk:(2 * g2 + 2) * blk]
            o_ref[qb * blk:(qb + 1) * blk, pair * LANES:(pair + 1) * LANES] = (
                jnp.where(first, even, odd).astype(o_ref.dtype))


def _swa_attention(qkv, sinks, *, n_q_groups):
    b, s, _ = qkv.shape
    blk = SW_BLOCK
    step = SW_STEP_BLOCKS * blk
    d_q = n_q_groups * LANES
    kv_w = SW_KV_HEADS * LANES
    k_col = d_q // kv_w
    v_col = k_col + 1
    cur = lambda col: (lambda bi, n: (bi, n, col))
    prev = lambda col: (lambda bi, n: (bi, jnp.maximum(n * SW_STEP_BLOCKS - 1, 0), col))
    return pl.pallas_call(
        _swa_attn_kernel,
        grid=(b, s // step),
        in_specs=[
            pl.BlockSpec((None, step, d_q), cur(0)),
            pl.BlockSpec((None, step, kv_w), cur(k_col)),
            pl.BlockSpec((None, blk, kv_w), prev(k_col)),
            pl.BlockSpec((None, step, kv_w), cur(v_col)),
            pl.BlockSpec((None, blk, kv_w), prev(v_col)),
            _resident((SW_KV_HEADS, SW_GROUP, 1, 1)),
        ],
        out_specs=pl.BlockSpec((None, step, d_q), cur(0)),
        out_shape=jax.ShapeDtypeStruct((b, s, d_q), BF16),
        compiler_params=_params(2),
        name="swa_attention",
    )(qkv, qkv, qkv, qkv, qkv, sinks)


def _dup_heads(w, n_heads):
    d = w.shape[0]
    w = w.reshape(d, n_heads, 1, HEAD_DIM)
    return jnp.broadcast_to(w, (d, n_heads, LANES // HEAD_DIM, HEAD_DIM)).reshape(d, n_heads * LANES)


def kernel(x, positions, norm_gains, ffn_w_gate_up, ffn_w_down, sb_w_in, sb_w_out,
           sw_w_in, sw_w_out, sw_q_gain, sw_k_gain, sw_sinks):
    b, s, d = x.shape
    m = b * s
    depth = norm_gains.shape[0]
    sb_heads = sb_w_out.shape[1] // HEAD_DIM
    d_q = sw_w_out.shape[1]
    n_q_groups = d_q // LANES
    d_kv = SW_KV_HEADS * HEAD_DIM

    cos_t, sin_up, sin_dn = _rope_tables(positions)
    idx = jnp.arange(SB_BLOCK)
    tri = jnp.where(idx[:, None] >= idx[None, :], -1.0, 0.0).astype(BF16)
    lane_head = jnp.arange(LANES) // HEAD_DIM
    seg = (lane_head[:, None] == lane_head[None, :]).astype(BF16)
    seg2 = jnp.concatenate([seg, seg], axis=0)

    w_gate_up = ffn_w_gate_up.astype(BF16)
    w_down = ffn_w_down.astype(BF16)
    w_out = (sb_w_out.astype(BF16), sw_w_out.astype(BF16))

    h = x.reshape(m, d)
    for i in range(depth):
        slot = i // 2
        gains = norm_gains[i].reshape(3, 1, d)
        h = _ffn(h, gains[0], w_gate_up, w_down, i, 0)
        if i % 2 == 0:
            qkv = _norm_proj(h, gains[1], sb_w_in[slot].astype(BF16),
                             n_scaled=sb_heads * HEAD_DIM, scale=HEAD_DIM ** -0.5 * LOG2_E)
            o = _sb_attention(qkv.reshape(b, s, -1), tri, n_heads=sb_heads)
        else:
            w_in = sw_w_in[slot]
            w_ext = jnp.concatenate(
                [w_in[:, :d_q],
                 _dup_heads(w_in[:, d_q:d_q + d_kv], SW_KV_HEADS),
                 _dup_heads(w_in[:, d_q + d_kv:], SW_KV_HEADS)], axis=1).astype(BF16)
            reps = LANES // HEAD_DIM
            head_gains = jnp.stack([jnp.tile(sw_q_gain[slot], reps), jnp.tile(sw_k_gain[slot], reps)])
            qkv = _sw_proj(h, gains[1], w_ext, cos_t, sin_up, sin_dn, head_gains, seg2,
                           n_q_groups=n_q_groups, n_k_groups=SW_KV_HEADS)
            sinks = sw_sinks[slot].reshape(SW_KV_HEADS, SW_GROUP, 1, 1)
            o = _swa_attention(qkv.reshape(b, s, -1), sinks, n_q_groups=n_q_groups)
        h = _ffn(h, gains[2], w_gate_up, w_down, i, 1,
                 attn=o.reshape(m, -1), w_out=w_out[i % 2], slot=slot)
    return h.reshape(b, s, d)
```

```python
import functools

import jax
import jax.numpy as jnp
from jax import lax
from jax.experimental import pallas as pl
from jax.experimental.pallas import tpu as pltpu

F32 = jnp.float32
BF16 = jnp.bfloat16

HEAD_DIM = 64
LANES = 128
SW_KV_HEADS = 2
SW_GROUP = 8
SW_BLOCK = 128
SW_STEP_BLOCKS = 4
SB_BLOCK = 256
SB_DIAG_GROUP = 4
SB_GROUP = 1
SB_STEPS = 24
ROPE_THETA = 500000.0
ROT_DIM = HEAD_DIM // 4
EPS = 1e-6
LOG2_E = 1.4426950408889634
VMEM_LIMIT_BYTES = 56 * 1024 * 1024
ROW_TILE = 512
FFN_ROW_TILE = 1024
FF_TILE = 256


def _params(n_axes):
    return pltpu.CompilerParams(
        dimension_semantics=("arbitrary",) * n_axes,
        vmem_limit_bytes=VMEM_LIMIT_BYTES)


def _resident(shape):
    zeros = (0,) * len(shape)
    return pl.BlockSpec(shape, lambda *_: zeros, pipeline_mode=pl.Buffered(1))


def _rms_norm_rows(x, gain):
    ms = jnp.mean(x * x, axis=-1, keepdims=True)
    return x * lax.rsqrt(ms + EPS) * gain


def _dot(a, b):
    return jnp.dot(a, b, preferred_element_type=F32)


def _dot_nt(a, b):
    return lax.dot_general(a, b, (((1,), (1,)), ((), ())), preferred_element_type=F32)


def _neg_abs(x):
    bits = lax.bitcast_convert_type(x, jnp.uint32) | jnp.uint32(0x80000000)
    return lax.bitcast_convert_type(bits, F32)


def _split_bf16(x):
    hi = x.astype(BF16)
    lo = (x - hi.astype(F32)).astype(BF16)
    return hi, lo


def _ffn_kernel(*refs, d_ff, has_attn):
    if has_attn:
        a_ref, wo_ref, x_ref, g_ref, wgu_ref, wd_ref, o_ref = refs
        x = x_ref[...] + _dot(a_ref[...], wo_ref[...])
    else:
        x_ref, g_ref, wgu_ref, wd_ref, o_ref = refs
        x = x_ref[...]
    h = _rms_norm_rows(x, g_ref[...]).astype(BF16)
    y = jnp.zeros_like(x)
    for c in range(d_ff // FF_TILE):
        lo = c * FF_TILE
        gate = _dot(h, wgu_ref[:, lo:lo + FF_TILE])
        up = _dot(h, wgu_ref[:, d_ff + lo:d_ff + lo + FF_TILE])
        act = gate * (1.0 / (1.0 + jnp.exp(-gate))) * up
        y = y + _dot(act.astype(BF16), wd_ref[lo:lo + FF_TILE, :])
    o_ref[...] = x + 0.5 * y


def _layer_weights(shape, *index):
    lead = len(index)
    return pl.BlockSpec((None,) * lead + tuple(shape), lambda *_: tuple(index) + (0, 0),
                        pipeline_mode=pl.Buffered(1))


def _ffn(x, gain, w_gate_up, w_down, layer, which, attn=None, w_out=None, slot=None):
    m, d = x.shape
    d_ff = w_down.shape[-2]
    row = lambda i: (i, 0)
    in_specs, args = [], []
    if attn is not None:
        k = attn.shape[1]
        in_specs += [pl.BlockSpec((FFN_ROW_TILE, k), row), _layer_weights((k, d), slot)]
        args += [attn, w_out]
    in_specs += [
        pl.BlockSpec((FFN_ROW_TILE, d), row),
        _resident((1, d)),
        _layer_weights((d, 2 * d_ff), layer, which),
        _layer_weights((d_ff, d), layer, which),
    ]
    args += [x, gain, w_gate_up, w_down]
    return pl.pallas_call(
        functools.partial(_ffn_kernel, d_ff=d_ff, has_attn=attn is not None),
        grid=(m // FFN_ROW_TILE,),
        in_specs=in_specs,
        out_specs=pl.BlockSpec((FFN_ROW_TILE, d), row),
        out_shape=jax.ShapeDtypeStruct((m, d), F32),
        compiler_params=_params(1),
        name="ffn",
    )(*args)


def _norm_proj_kernel(x_ref, g_ref, w_ref, o_ref, *, n_scaled, scale):
    h = _rms_norm_rows(x_ref[...], g_ref[...]).astype(BF16)
    y = _dot(h, w_ref[...])
    o_ref[:, :n_scaled] = (y[:, :n_scaled] * scale).astype(o_ref.dtype)
    o_ref[:, n_scaled:] = y[:, n_scaled:].astype(o_ref.dtype)


def _norm_proj(x, gain, w, *, n_scaled, scale):
    m, d = x.shape
    n = w.shape[1]
    return pl.pallas_call(
        functools.partial(_norm_proj_kernel, n_scaled=n_scaled, scale=scale),
        grid=(m // ROW_TILE,),
        in_specs=[
            pl.BlockSpec((ROW_TILE, d), lambda i: (i, 0)),
            _resident((1, d)),
            _resident((d, n)),
        ],
        out_specs=pl.BlockSpec((ROW_TILE, n), lambda i: (i, 0)),
        out_shape=jax.ShapeDtypeStruct((m, n), BF16),
        compiler_params=_params(1),
        name="norm_proj",
    )(x, gain, w)


def _sb_attn_kernel(tq_ref, tj_ref, ts_ref, q_ref, k_ref, v_ref, tri_ref, o_ref,
                    acc_ref, c_ref, qs_ref, z0_ref, z1_ref, a0_ref, a1_ref, *, n_blk, n_groups):
    blk = SB_BLOCK
    lane = lax.broadcasted_iota(jnp.int32, (blk, LANES), 1)
    first = lane < HEAD_DIM

    def logits(qis, kjs):
        return [_dot_nt(qs_ref[i], k_ref[pl.ds(pl.multiple_of(j * blk, blk), blk), :])
                for i, j in zip(qis, kjs)]

    def log_weights(zs, carries, causal):
        sps = [jnp.maximum(z, 0.0) + jnp.log2(1.0 + jnp.exp2(_neg_abs(z))) for z in zs]
        if causal is not None:
            sps = [jnp.where(causal, sp, 0.0) for sp in sps]
        laters = [_dot(sp.astype(BF16), tri_ref[...]) for sp in sps]
        args = [z + later for z, later in zip(zs, laters)]
        if carries is not None:
            args = [arg + jnp.concatenate([carry] * (blk // LANES), axis=1)
                    for arg, carry in zip(args, carries)]
        if causal is not None:
            args = [jnp.where(causal, arg, -jnp.inf) for arg in args]
        return args, [jnp.broadcast_to(later[:, 0:1], (2 * blk, LANES)) for later in laters]

    def values(args, kjs):
        vbs = [v_ref[pl.ds(pl.multiple_of(j * blk, blk), blk), :] for j in kjs]
        return [_dot(jnp.exp2(arg).astype(BF16), vb) for arg, vb in zip(args, vbs)]

    q = q_ref[...].reshape(n_blk, blk, LANES)
    zero = jnp.zeros_like(q)
    qs_ref[:, 0:blk, :] = jnp.where(first[None], q, zero)
    qs_ref[:, blk:2 * blk, :] = jnp.where(first[None], zero, q)

    row = lax.broadcasted_iota(jnp.int32, (2 * blk, blk), 0)
    row = jnp.where(row >= blk, row - blk, row)
    col = lax.broadcasted_iota(jnp.int32, (2 * blk, blk), 1)
    causal = col < row

    def diag_body(it, _):
        idx = [it * SB_DIAG_GROUP + u for u in range(SB_DIAG_GROUP)]
        args, totals = log_weights(logits(idx, idx), None, causal)
        pvs = values(args, idx)
        for i, pv, total in zip(idx, pvs, totals):
            acc_ref[i] = pv
            c_ref[i] = total
        return 0

    lax.fori_loop(0, n_blk // SB_DIAG_GROUP, diag_body, 0)
    acc_ref[n_blk] = jnp.zeros(acc_ref.shape[1:], F32)
    c_ref[n_blk] = jnp.zeros(c_ref.shape[1:], F32)

    def group(g):
        return [g * SB_GROUP + u for u in range(SB_GROUP)]

    def logit_stage(g, zw_ref):
        ts = group(g)
        for u, z in enumerate(logits([tq_ref[t] for t in ts], [tj_ref[t] for t in ts])):
            zw_ref[u] = z

    def weight_stage(g, zr_ref, aw_ref):
        slots = [ts_ref[t] for t in group(g)]
        carries = [c_ref[slot] for slot in slots]
        args, totals = log_weights([zr_ref[u] for u in range(SB_GROUP)], carries, None)
        for u, (slot, carry, arg, total) in enumerate(zip(slots, carries, args, totals)):
            aw_ref[u] = arg
            c_ref[slot] = carry + total

    def value_stage(g, ar_ref):
        ts = group(g)
        pvs = values([ar_ref[u] for u in range(SB_GROUP)], [tj_ref[t] for t in ts])
        for t, pv in zip(ts, pvs):
            acc_ref[ts_ref[t]] += pv

    z_refs = (z0_ref, z1_ref)
    a_refs = (a0_ref, a1_ref)

    def step(g, parity):
        logit_stage(g, z_refs[parity])
        value_stage(g - 2, a_refs[parity])
        weight_stage(g - 1, z_refs[1 - parity], a_refs[1 - parity])

    logit_stage(0, z_refs[0])
    logit_stage(1, z_refs[1])
    weight_stage(0, z_refs[0], a_refs[0])
    n_bodies = (n_groups - 2) // SB_STEPS

    def off_body(h, _):
        for k in range(SB_STEPS):
            step(2 + SB_STEPS * h + k, k % 2)
        return 0

    lax.fori_loop(0, n_bodies, off_body, 0)
    for g in range(2 + SB_STEPS * n_bodies, n_groups):
        step(g, g % 2)
    last = (n_groups - 1) % 2
    weight_stage(n_groups - 1, z_refs[last], a_refs[last])
    value_stage(n_groups - 2, a_refs[1 - last])
    value_stage(n_groups - 1, a_refs[last])

    acc = acc_ref[0:n_blk]
    o = jnp.where(first[None], acc[:, :blk], acc[:, blk:])
    o_ref[...] = o.reshape(n_blk * blk, LANES).astype(o_ref.dtype)


def _sb_tile_order(n_blk):
    dummy = (0, 0, n_blk)
    order = []
    for d in range(1, n_blk):
        diag = [(i, i - d, i) for i in range(d, n_blk)]
        room = -len(order) % SB_GROUP
        if any(i in {t[2] for t in order[len(order) - (SB_GROUP - room):]} for i, _, _ in diag[:room]):
            order += [dummy] * room
        order += diag
    order += [dummy] * (-len(order) % SB_GROUP)
    while len(order) // SB_GROUP < 2 or (len(order) // SB_GROUP - 2) % SB_STEPS:
        order += [dummy] * SB_GROUP
    return order


def _sb_attention(qkv, tri, *, n_heads):
    b, s, _ = qkv.shape
    n_pairs = n_heads * HEAD_DIM // LANES
    blk = SB_BLOCK
    n_blk = s // blk
    assert n_blk % SB_DIAG_GROUP == 0 and SB_STEPS % 2 == 0
    order = _sb_tile_order(n_blk)
    tables = [jnp.array([t[c] for t in order], jnp.int32) for c in range(3)]
    seq = lambda col: pl.BlockSpec((None, s, LANES), lambda bi, h, *_: (bi, 0, col * n_pairs + h))
    grid_spec = pltpu.PrefetchScalarGridSpec(
        num_scalar_prefetch=3,
        grid=(b, n_pairs),
        in_specs=[
            seq(0), seq(1), seq(2),
            pl.BlockSpec((blk, blk), lambda bi, h, *_: (0, 0), pipeline_mode=pl.Buffered(1)),
        ],
        out_specs=pl.BlockSpec((None, s, LANES), lambda bi, h, *_: (bi, 0, h)),
        scratch_shapes=[
            pltpu.VMEM((n_blk + 1, 2 * blk, LANES), F32),
            pltpu.VMEM((n_blk + 1, 2 * blk, LANES), F32),
            pltpu.VMEM((n_blk, 2 * blk, LANES), BF16),
        ] + [pltpu.VMEM((SB_GROUP, 2 * blk, blk), F32)] * 4,
    )
    return pl.pallas_call(
        functools.partial(_sb_attn_kernel, n_blk=n_blk, n_groups=len(order) // SB_GROUP),
        grid_spec=grid_spec,
        out_shape=jax.ShapeDtypeStruct((b, s, n_pairs * LANES), BF16),
        compiler_params=_params(2),
        name="sb_attention",
    )(*tables, qkv, qkv, qkv, tri)


def _rope_table_kernel(pos_ref, invf_ref, cos_ref, sin_ref):
    ang = pos_ref[...].astype(F32) * invf_ref[...]
    cos_ref[...] = jnp.cos(ang)
    sin_ref[...] = jnp.sin(ang)


def _rope_tables(positions):
    b, s = positions.shape
    half = ROT_DIM // 2
    rows = s * half // LANES
    inv_freq = ROPE_THETA ** (-jnp.arange(0, ROT_DIM, 2, dtype=F32) / ROT_DIM)
    invf = jnp.tile(inv_freq, LANES // half).reshape(1, LANES)
    pos = jnp.repeat(positions, half, axis=1).reshape(b, rows, LANES)
    spec = pl.BlockSpec((None, rows, LANES), lambda i: (i, 0, 0))
    cos, sin = pl.pallas_call(
        _rope_table_kernel,
        grid=(b,),
        in_specs=[spec, _resident((1, LANES))],
        out_specs=[spec, spec],
        out_shape=[jax.ShapeDtypeStruct((b, rows, LANES), F32)] * 2,
        compiler_params=_params(1),
        name="rope_tables",
    )(pos, invf)
    cos = cos.reshape(b, s, half)
    sin = sin.reshape(b, s, half)
    rest = HEAD_DIM - ROT_DIM
    ones = jnp.ones((b, s, rest), F32)
    zeros = jnp.zeros((b, s, rest + half), F32)
    reps = LANES // HEAD_DIM
    cos_t = jnp.tile(jnp.concatenate([cos, cos, ones], axis=-1), (1, 1, reps))
    sin_up = jnp.tile(jnp.concatenate([sin, zeros], axis=-1), (1, 1, reps))
    sin_dn = jnp.tile(jnp.concatenate([zeros[..., :half], sin, zeros[..., :rest]], axis=-1),
                      (1, 1, reps))
    m = b * s
    return cos_t.reshape(m, LANES), sin_up.reshape(m, LANES), sin_dn.reshape(m, LANES)


def _sw_proj_kernel(x_ref, g_ref, w_ref, cos_ref, sup_ref, sdn_ref, hg_ref, seg_ref, o_ref,
                    *, n_q_groups, n_k_groups, n_groups):
    h = _rms_norm_rows(x_ref[...], g_ref[...]).astype(BF16)
    y = _dot(h, w_ref[...])
    cos_t = cos_ref[...]
    sin_up = sup_ref[...]
    sin_dn = sdn_ref[...]
    seg2 = seg_ref[...]
    for grp in range(n_groups):
        yg = y[:, grp * LANES:(grp + 1) * LANES]
        if grp < n_q_groups + n_k_groups:
            is_q = grp < n_q_groups
            hi, lo = _split_bf16(yg * yg)
            ss = _dot(jnp.concatenate([hi, lo], axis=1), seg2)
            gain = hg_ref[0:1, :] if is_q else hg_ref[1:2, :]
            yn = yg * lax.rsqrt(ss * (1.0 / HEAD_DIM) + EPS) * gain
            up = pltpu.roll(yn, LANES - ROT_DIM // 2, 1)
            dn = pltpu.roll(yn, ROT_DIM // 2, 1)
            r = yn * cos_t - up * sin_up + dn * sin_dn
            if is_q:
                r = r * (HEAD_DIM ** -0.5)
            o_ref[:, grp * LANES:(grp + 1) * LANES] = r.astype(o_ref.dtype)
        else:
            o_ref[:, grp * LANES:(grp + 1) * LANES] = yg.astype(o_ref.dtype)


def _sw_proj(x, gain, w, cos_t, sin_up, sin_dn, head_gains, seg2, *, n_q_groups, n_k_groups):
    m, d = x.shape
    n = w.shape[1]
    row = lambda i: (i, 0)
    return pl.pallas_call(
        functools.partial(_sw_proj_kernel, n_q_groups=n_q_groups, n_k_groups=n_k_groups,
                          n_groups=n // LANES),
        grid=(m // ROW_TILE,),
        in_specs=[
            pl.BlockSpec((ROW_TILE, d), row),
            _resident((1, d)),
            _resident((d, n)),
            pl.BlockSpec((ROW_TILE, LANES), row),
            pl.BlockSpec((ROW_TILE, LANES), row),
            pl.BlockSpec((ROW_TILE, LANES), row),
            _resident((2, LANES)),
            _resident((2 * LANES, LANES)),
        ],
        out_specs=pl.BlockSpec((ROW_TILE, n), row),
        out_shape=jax.ShapeDtypeStruct((m, n), BF16),
        compiler_params=_params(1),
        name="sw_proj",
    )(x, gain, w, cos_t, sin_up, sin_dn, head_gains, seg2)


def _swa_attn_kernel(q_ref, kc_ref, kp_ref, vc_ref, vp_ref, sink_ref, o_ref):
    blk = SW_BLOCK
    n = pl.program_id(1)
    lane = lax.broadcasted_iota(jnp.int32, (blk, LANES), 1)
    first = lane < HEAD_DIM
    t = lax.broadcasted_iota(jnp.int32, (blk, 2 * blk), 0)
    j = lax.broadcasted_iota(jnp.int32, (blk, 2 * blk), 1)
    band = (j > t) & (j <= t + blk)
    ones = jnp.ones((2 * blk, LANES), BF16)
    chains = [(qb, kh) for qb in range(SW_STEP_BLOCKS) for kh in range(SW_KV_HEADS)]
    masks, kbs, vbs, lhs = [], [], [], []
    for qb, kh in chains:
        cols = slice(kh * LANES, (kh + 1) * LANES)
        rows_cur = slice(qb * blk, (qb + 1) * blk)
        if qb == 0:
            k_prev, v_prev = kp_ref[:, cols], vp_ref[:, cols]
            masks.append(band & ((j >= blk) | (n > 0)))
        else:
            rows_prev = slice((qb - 1) * blk, qb * blk)
            k_prev, v_prev = kc_ref[rows_prev, cols], vc_ref[rows_prev, cols]
            masks.append(band)
        kbs.append(jnp.concatenate([k_prev, kc_ref[rows_cur, cols]], axis=0))
        vbs.append(jnp.concatenate([v_prev, vc_ref[rows_cur, cols]], axis=0))
        rows = []
        for g in range(SW_GROUP):
            pair = (kh * SW_GROUP + g) // 2
            qp = q_ref[rows_cur, pair * LANES:(pair + 1) * LANES]
            zero = jnp.zeros_like(qp)
            rows.append(jnp.where(first, qp, zero) if g % 2 == 0 else jnp.where(first, zero, qp))
        lhs.append(jnp.concatenate(rows, axis=0))
    ss = [_dot_nt(l, kb).reshape(SW_GROUP, blk, 2 * blk) for l, kb in zip(lhs, kbs)]
    ss = [jnp.where(mask[None], s, -jnp.inf) for s, mask in zip(ss, masks)]
    sinks = [sink_ref[kh] for _, kh in chains]
    ms = [jnp.maximum(jnp.max(s, axis=-1, keepdims=True), sink) for s, sink in zip(ss, sinks)]
    ms = [jnp.broadcast_to(m, (SW_GROUP, blk, LANES)) for m in ms]
    ps = [jnp.exp(s - jnp.concatenate([m, m], axis=-1)).astype(BF16).reshape(SW_GROUP * blk, 2 * blk)
          for s, m in zip(ss, ms)]
    denoms = [_dot(p, ones) + jnp.exp(sink - m).reshape(SW_GROUP * blk, LANES)
              for p, sink, m in zip(ps, sinks, ms)]
    outs = [_dot(p, vb) / denom for p, vb, denom in zip(ps, vbs, denoms)]
    for (qb, kh), o in zip(chains, outs):
        for g2 in range(SW_GROUP // 2):
            pair = kh * (SW_GROUP // 2) + g2
            even = o[(2 * g2) * blk:(2 * g2 + 1) * blk]
            odd = o[(2 * g2 + 1) * blk:(2 * g2 + 2) * blk]
            o_ref[qb * blk:(qb + 1) * blk, pair * LANES:(pair + 1) * LANES] = (
                jnp.where(first, even, odd).astype(o_ref.dtype))


def _swa_attention(qkv, sinks, *, n_q_groups):
    b, s, _ = qkv.shape
    blk = SW_BLOCK
    step = SW_STEP_BLOCKS * blk
    d_q = n_q_groups * LANES
    kv_w = SW_KV_HEADS * LANES
    k_col = d_q // kv_w
    v_col = k_col + 1
    cur = lambda col: (lambda bi, n: (bi, n, col))
    prev = lambda col: (lambda bi, n: (bi, jnp.maximum(n * SW_STEP_BLOCKS - 1, 0), col))
    return pl.pallas_call(
        _swa_attn_kernel,
        grid=(b, s // step),
        in_specs=[
            pl.BlockSpec((None, step, d_q), cur(0)),
            pl.BlockSpec((None, step, kv_w), cur(k_col)),
            pl.BlockSpec((None, blk, kv_w), prev(k_col)),
            pl.BlockSpec((None, step, kv_w), cur(v_col)),
            pl.BlockSpec((None, blk, kv_w), prev(v_col)),
            _resident((SW_KV_HEADS, SW_GROUP, 1, 1)),
        ],
        out_specs=pl.BlockSpec((None, step, d_q), cur(0)),
        out_shape=jax.ShapeDtypeStruct((b, s, d_q), BF16),
        compiler_params=_params(2),
        name="swa_attention",
    )(qkv, qkv, qkv, qkv, qkv, sinks)


def _dup_heads(w, n_heads):
    d = w.shape[0]
    w = w.reshape(d, n_heads, 1, HEAD_DIM)
    return jnp.broadcast_to(w, (d, n_heads, LANES // HEAD_DIM, HEAD_DIM)).reshape(d, n_heads * LANES)


def kernel(x, positions, norm_gains, ffn_w_gate_up, ffn_w_down, sb_w_in, sb_w_out,
           sw_w_in, sw_w_out, sw_q_gain, sw_k_gain, sw_sinks):
    b, s, d = x.shape
    m = b * s
    depth = norm_gains.shape[0]
    sb_heads = sb_w_out.shape[1] // HEAD_DIM
    d_q = sw_w_out.shape[1]
    n_q_groups = d_q // LANES
    d_kv = SW_KV_HEADS * HEAD_DIM

    cos_t, sin_up, sin_dn = _rope_tables(positions)
    idx = jnp.arange(SB_BLOCK)
    tri = jnp.where(idx[:, None] >= idx[None, :], -1.0, 0.0).astype(BF16)
    lane_head = jnp.arange(LANES) // HEAD_DIM
    seg = (lane_head[:, None] == lane_head[None, :]).astype(BF16)
    seg2 = jnp.concatenate([seg, seg], axis=0)

    w_gate_up = ffn_w_gate_up.astype(BF16)
    w_down = ffn_w_down.astype(BF16)
    w_out = (sb_w_out.astype(BF16), sw_w_out.astype(BF16))

    h = x.reshape(m, d)
    for i in range(depth):
        slot = i // 2
        gains = norm_gains[i].reshape(3, 1, d)
        h = _ffn(h, gains[0], w_gate_up, w_down, i, 0)
        if i % 2 == 0:
            qkv = _norm_proj(h, gains[1], sb_w_in[slot].astype(BF16),
                             n_scaled=sb_heads * HEAD_DIM, scale=HEAD_DIM ** -0.5 * LOG2_E)
            o = _sb_attention(qkv.reshape(b, s, -1), tri, n_heads=sb_heads)
        else:
            w_in = sw_w_in[slot]
            w_ext = jnp.concatenate(
                [w_in[:, :d_q],
                 _dup_heads(w_in[:, d_q:d_q + d_kv], SW_KV_HEADS),
                 _dup_heads(w_in[:, d_q + d_kv:], SW_KV_HEADS)], axis=1).astype(BF16)
            reps = LANES // HEAD_DIM
            head_gains = jnp.stack([jnp.tile(sw_q_gain[slot], reps), jnp.tile(sw_k_gain[slot], reps)])
            qkv = _sw_proj(h, gains[1], w_ext, cos_t, sin_up, sin_dn, head_gains, seg2,
                           n_q_groups=n_q_groups, n_k_groups=SW_KV_HEADS)
            sinks = sw_sinks[slot].reshape(SW_KV_HEADS, SW_GROUP, 1, 1)
            o = _swa_attention(qkv.reshape(b, s, -1), sinks, n_q_groups=n_q_groups)
        h = _ffn(h, gains[2], w_gate_up, w_down, i, 1,
                 attn=o.reshape(m, -1), w_out=w_out[i % 2], slot=slot)
    return h.reshape(b, s, d)
```
